```python
import jax, jax.numpy as jnp
from jax import lax
import numpy as np

D_MODEL = 2048
BATCH = 16
SEQ = 256
DEPTH = 2
DEC_BATCH = 4
DEC_SEQ = 4096
PAST_LEN = 256

GRID_W = 64
HEAD_DIM = 128
N_BRANCH = 4
BRANCH_W = 512
CONV_W = BRANCH_W
CONV_K = 3
NAT_HEADS = BRANCH_W // HEAD_DIM
NAT_KR = 8
NAT_KC = 16
GQA_Q_HEADS = BRANCH_W // HEAD_DIM
GQA_KV_HEADS = 2
RET_HEADS = 4
RET_DK = 128
RET_DV = 128
RET_CHUNK = 128
Q_BLOCK = 128
ROPE_THETA = 10000.0
N_EXPERTS = 16
N_GROUPS = 4
EXPERTS_PER_GROUP = N_EXPERTS // N_GROUPS
TOP_K = 2
D_EXPERT = 1024
MOE_BLOCK = 128
EPS = 1e-6
IN_WIDTHS = (CONV_W, CONV_W, CONV_W,
             NAT_HEADS * HEAD_DIM, NAT_HEADS * HEAD_DIM, NAT_HEADS * HEAD_DIM,
             GQA_Q_HEADS * HEAD_DIM, GQA_KV_HEADS * HEAD_DIM, GQA_KV_HEADS * HEAD_DIM,
             RET_HEADS * RET_DK, RET_HEADS * RET_DK, RET_HEADS * RET_DV, RET_HEADS * RET_DV,
             N_BRANCH * D_MODEL)
IN_W = sum(IN_WIDTHS)

kernel_name = 'hybrid_diffusion_prefix_trunk_step'


def rms_norm(x, g):
    xf = x.astype(jnp.float32)
    y = xf * lax.rsqrt(jnp.mean(xf * xf, axis=-1, keepdims=True) + EPS)
    return (y * g.astype(jnp.float32)).astype(x.dtype)


def heads(a, n):
    B, T, _ = a.shape
    return a.reshape(B, T, n, -1).transpose(0, 2, 1, 3)


def merge_heads(a):
    B, n, T, d = a.shape
    return a.transpose(0, 2, 1, 3).reshape(B, T, n * d)


def modulation(cvec, w_mod, b_mod):
    m = jax.nn.silu(cvec) @ w_mod + b_mod
    return jnp.split(m, 6, axis=-1)


def modulate(x, shift, scale):
    return x * (1 + scale) + shift


def axial_rope_tables(n_tok):
    t = jnp.arange(n_tok)
    row = (t // GRID_W).astype(jnp.float32)
    col = (t % GRID_W).astype(jnp.float32)
    quarter = HEAD_DIM // 4
    inv = ROPE_THETA ** (-jnp.arange(quarter, dtype=jnp.float32) / quarter)
    ar = row[:, None] * inv
    ac = col[:, None] * inv
    ang = jnp.concatenate([ar, ar, ac, ac], axis=-1)
    return jnp.cos(ang), jnp.sin(ang)


def apply_axial_rope(x, cos, sin):
    xf = x.astype(jnp.float32)
    a1, a2, b1, b2 = jnp.split(xf, 4, axis=-1)
    rot = jnp.concatenate([-a2, a1, -b2, b1], axis=-1)
    return (xf * cos + rot * sin).astype(x.dtype)


def block_attention(q, k, v):
    B, Hk, G, Nq, dh = q.shape
    nb = Nq // Q_BLOCK
    qb = q.reshape(B, Hk, G, nb, Q_BLOCK, dh).transpose(3, 0, 1, 2, 4, 5)
    scale = dh ** -0.5

    def one_block(qi):
        s = jnp.einsum('bkgqd,bkmd->bkgqm', qi, k).astype(jnp.float32) * scale
        p = jax.nn.softmax(s, axis=-1).astype(v.dtype)
        return jnp.einsum('bkgqm,bkmd->bkgqd', p, v)

    o = lax.map(one_block, qb)
    return o.transpose(1, 2, 3, 0, 4, 5).reshape(B, Hk, G, Nq, dh)


def gqa_attend(q, k, v):
    B, Hq, Nq, dh = q.shape
    Hk = k.shape[1]
    o = block_attention(q.reshape(B, Hk, Hq // Hk, Nq, dh), k, v)
    return o.reshape(B, Hq, Nq, dh)


def natten_latent(q, k, v, k_ctx, v_ctx, rpb):
    B, H, N, dh = q.shape
    rows = N // GRID_W
    kr = min(NAT_KR, rows)
    qg = q.reshape(B, H, rows, GRID_W, dh)
    kg = k.reshape(B, H, rows, GRID_W, dh)
    vg = v.reshape(B, H, rows, GRID_W, dh)
    row_start = jnp.clip(jnp.arange(rows) - kr // 2, 0, rows - kr)
    cols = jnp.arange(GRID_W)
    col_idx = jnp.clip(cols - NAT_KC // 2, 0, GRID_W - NAT_KC)[:, None] + jnp.arange(NAT_KC)
    col_bias_idx = col_idx - cols[:, None] + NAT_KC - 1
    n_loc = kr * NAT_KC
    scale = dh ** -0.5

    def row_block(r):
        rs = row_start[r]
        qr = lax.dynamic_index_in_dim(qg, r, axis=2, keepdims=False)
        kb = lax.dynamic_slice_in_dim(kg, rs, kr, axis=2)
        vb = lax.dynamic_slice_in_dim(vg, rs, kr, axis=2)
        kw = kb[:, :, :, col_idx]
        vw = vb[:, :, :, col_idx]
        row_bias_idx = rs + jnp.arange(kr) - r + NAT_KR - 1
        bias = rpb[:, row_bias_idx][:, :, col_bias_idx].transpose(0, 2, 1, 3)
        s_loc = jnp.einsum('bhqd,bhrqcd->bhqrc', qr, kw).astype(jnp.float32) * scale + bias.astype(jnp.float32)
        s_ctx = jnp.einsum('bhqd,bhkd->bhqk', qr, k_ctx).astype(jnp.float32) * scale
        s = jnp.concatenate([s_loc.reshape(B, H, GRID_W, n_loc), s_ctx], axis=-1)
        p = jax.nn.softmax(s, axis=-1).astype(v.dtype)
        p_loc = p[..., :n_loc].reshape(B, H, GRID_W, kr, NAT_KC)
        return (jnp.einsum('bhqrc,bhrqcd->bhqd', p_loc, vw)
                + jnp.einsum('bhqk,bhkd->bhqd', p[..., n_loc:], v_ctx))

    o = lax.map(row_block, jnp.arange(rows))
    return o.transpose(1, 2, 0, 3, 4).reshape(B, H, N, dh)


def short_conv(z, w):
    return lax.conv_general_dilated(z, w[:, None, :], window_strides=(1,),
                                    padding=((CONV_K // 2, CONV_K // 2),),
                                    dimension_numbers=('NWC', 'WIO', 'NWC'),
                                    feature_group_count=z.shape[-1])


def retention_scan(q, k, v, log_gamma, s0):
    B, H, T, _ = q.shape
    dv = v.shape[-1]
    nc = T // RET_CHUNK

    def chunks(a):
        return a.astype(jnp.float32).reshape(B, H, nc, RET_CHUNK, a.shape[-1]).transpose(2, 0, 1, 3, 4)

    idx = jnp.arange(RET_CHUNK, dtype=jnp.float32)
    lg = log_gamma.astype(jnp.float32)[:, None, None]
    rel = idx[:, None] - idx[None, :]
    intra = jnp.where(rel >= 0, jnp.exp(lg * jnp.maximum(rel, 0.0)), 0.0)
    q_dec = jnp.exp(lg[:, :, 0] * (idx + 1.0))[:, :, None]
    k_dec = jnp.exp(lg[:, :, 0] * (RET_CHUNK - 1.0 - idx))[:, :, None]
    c_dec = jnp.exp(lg * RET_CHUNK)

    def step(s, qkv):
        qc, kc, vc = qkv
        att = jnp.einsum('bhid,bhjd->bhij', qc, kc) * intra
        o = jnp.einsum('bhij,bhjv->bhiv', att, vc) + jnp.einsum('bhid,bhdv->bhiv', qc * q_dec, s)
        s = s * c_dec + jnp.einsum('bhjd,bhjv->bhdv', kc * k_dec, vc)
        return s, o

    s_fin, o = lax.scan(step, s0.astype(jnp.float32), (chunks(q), chunks(k), chunks(v)))
    return o.transpose(1, 2, 0, 3, 4).reshape(B, H, T, dv), s_fin


def bidir_retention(rq, rk, rv, rg, lp, s_f0, s_b0):
    lgf = jax.nn.log_sigmoid(lp['ret_decay_fwd'].astype(jnp.float32))
    lgb = jax.nn.log_sigmoid(lp['ret_decay_bwd'].astype(jnp.float32))
    o_f, s_f = retention_scan(rq, rk, rv, lgf, s_f0)
    o_b, s_b = retention_scan(jnp.flip(rq, 2), jnp.flip(rk, 2), jnp.flip(rv, 2), lgb, s_b0)
    o = o_f + jnp.flip(o_b, 2)
    o = o * lax.rsqrt(jnp.mean(o * o, axis=-1, keepdims=True) + EPS)
    y = jax.nn.silu(rg) * (merge_heads(o).astype(rg.dtype) * lp['ret_gn'])
    return y, s_f, s_b


def project(h, lp):
    cuts = np.cumsum(IN_WIDTHS)[:-1].tolist()
    (u, bg, cg, nq, nk, nv, gq, gk, gv, rq, rk, rv, rg, gl) = jnp.split(h @ lp['w_in'], cuts, axis=-1)
    nat = (rms_norm(heads(nq, NAT_HEADS), lp['nat_qn']), rms_norm(heads(nk, NAT_HEADS), lp['nat_kn']),
           heads(nv, NAT_HEADS))
    gqa = (rms_norm(heads(gq, GQA_Q_HEADS), lp['gqa_qn']), rms_norm(heads(gk, GQA_KV_HEADS), lp['gqa_kn']),
           heads(gv, GQA_KV_HEADS))
    ret = (heads(rq, RET_HEADS), heads(rk, RET_HEADS) * RET_DK ** -0.5, heads(rv, RET_HEADS), rg)
    return (u, bg, cg), nat, gqa, ret, gl


def merge_branches(branches, gate_logits, w_branch, w_out):
    g = jax.nn.sigmoid(gate_logits.reshape(gate_logits.shape[:-1] + (N_BRANCH, D_MODEL)))
    m = g[..., 0, :] * (branches[0] @ w_branch[0])
    for i in range(1, N_BRANCH):
        m = m + g[..., i, :] * (branches[i] @ w_branch[i])
    return m @ w_out


def moe_ffn(h, w_router, b_router, w_gate, w_up, w_down):
    lead = h.shape[:-1]
    x = h.reshape(-1, D_MODEL)
    n = x.shape[0]
    scores = jax.nn.sigmoid(x.astype(jnp.float32) @ w_router.astype(jnp.float32))
    sel = (scores + b_router.astype(jnp.float32)).reshape(n, N_GROUPS, EXPERTS_PER_GROUP)
    group_score = jnp.sum(lax.top_k(sel, 2)[0], axis=-1)
    g_best = jnp.argmax(group_score, axis=-1)
    in_group = jnp.take_along_axis(sel, g_best[:, None, None], axis=1)[:, 0]
    _, local = lax.top_k(in_group, TOP_K)
    eid = g_best[:, None] * EXPERTS_PER_GROUP + local
    wts = jnp.take_along_axis(scores, eid, axis=1)
    wts = wts / jnp.sum(wts, axis=-1, keepdims=True)
    n_assign = n * TOP_K
    flat_e = eid.reshape(-1)
    order = jnp.argsort(flat_e)
    e_sorted = flat_e[order]
    tok_sorted = order // TOP_K
    w_sorted = wts.reshape(-1)[order]
    counts = jnp.bincount(flat_e, length=N_EXPERTS)
    padded = (counts + MOE_BLOCK - 1) // MOE_BLOCK * MOE_BLOCK
    pend = jnp.cumsum(padded)
    slot = (pend - padded)[e_sorted] + jnp.arange(n_assign) - (jnp.cumsum(counts) - counts)[e_sorted]
    n_blocks = -(-n_assign // MOE_BLOCK) + N_EXPERTS
    buf = jnp.zeros((n_blocks * MOE_BLOCK, D_MODEL), x.dtype).at[slot].set(x[tok_sorted])
    block_expert = jnp.minimum(jnp.searchsorted(pend, jnp.arange(n_blocks) * MOE_BLOCK, side='right'),
                               N_EXPERTS - 1)

    def expert_block(args):
        xb, e = args
        return (jax.nn.silu(xb @ w_gate[e]) * (xb @ w_up[e])) @ w_down[e]

    yb = lax.map(expert_block, (buf.reshape(n_blocks, MOE_BLOCK, D_MODEL), block_expert))
    y = yb.reshape(-1, D_MODEL)[slot] * w_sorted[:, None].astype(x.dtype)
    return jnp.zeros_like(x).at[tok_sorted].add(y).reshape(lead + (D_MODEL,))


def context_layer(x, cvec, lp, w_router, b_router):
    sh1, sc1, g1, sh2, sc2, g2 = modulation(cvec, lp['w_mod'], lp['b_mod'])
    h = modulate(rms_norm(x, lp['norm1']), sh1, sc1)
    (u, bg, cg), (nq, nk, nv), (gq, gk, gv), (rq, rk, rv, rg), gl = project(h, lp)
    y_conv = bg * short_conv(cg * u, lp['conv_w'])
    y_nat = merge_heads(gqa_attend(nq, nk, nv))
    y_gqa = merge_heads(gqa_attend(gq, gk, gv))
    s0 = jnp.zeros((x.shape[0], RET_HEADS, RET_DK, RET_DV), jnp.float32)
    y_ret, s_f, s_b = bidir_retention(rq, rk, rv, rg, lp, s0, s0)
    x = x + g1 * merge_branches((y_conv, y_nat, y_gqa, y_ret), gl, lp['w_branch'], lp['w_out'])
    h2 = modulate(rms_norm(x, lp['norm2']), sh2, sc2)
    x = x + g2 * moe_ffn(h2, w_router, b_router, lp['w_exp_gate'], lp['w_exp_up'], lp['w_exp_down'])
    return x, (nk, nv, gk, gv, s_f, s_b)


def latent_layer(x, cvec, ctx, lp, w_router, b_router, cos, sin):
    nat_kc, nat_vc, gqa_kc, gqa_vc, s_f0, s_b0 = ctx
    sh1, sc1, g1, sh2, sc2, g2 = modulation(cvec, lp['w_mod'], lp['b_mod'])
    h = modulate(rms_norm(x, lp['norm1']), sh1, sc1)
    (u, bg, cg), (nq, nk, nv), (gq, gk, gv), (rq, rk, rv, rg), gl = project(h, lp)
    y_conv = bg * short_conv(cg * u, lp['conv_w'])
    y_nat = merge_heads(natten_latent(nq, nk, nv, nat_kc, nat_vc, lp['nat_rpb']))
    k_all = jnp.concatenate([apply_axial_rope(gk, cos, sin), gqa_kc], axis=2)
    v_all = jnp.concatenate([gv, gqa_vc], axis=2)
    y_gqa = merge_heads(gqa_attend(apply_axial_rope(gq, cos, sin), k_all, v_all))
    y_ret, _, _ = bidir_retention(rq, rk, rv, rg, lp, s_f0, s_b0)
    x = x + g1 * merge_branches((y_conv, y_nat, y_gqa, y_ret), gl, lp['w_branch'], lp['w_out'])
    h2 = modulate(rms_norm(x, lp['norm2']), sh2, sc2)
    return x + g2 * moe_ffn(h2, w_router, b_router, lp['w_exp_gate'], lp['w_exp_up'], lp['w_exp_down'])


def setup_inputs(seed: int = 0) -> dict:
    key = jax.random.key(seed)
    ks = iter(jax.random.split(key, 40))

    def nrm(shape, s=1.0):
        return s * jax.random.normal(next(ks), shape, jnp.float32)

    base = 1.0 - 2.0 ** (-5.0 - jnp.arange(RET_HEADS, dtype=jnp.float32))
    base_logit = jnp.log(base) - jnp.log1p(-base)
    return {
        'x_prompt': nrm((BATCH, SEQ, D_MODEL)),
        'x_sample': nrm((DEC_BATCH, DEC_SEQ, D_MODEL)),
        'cache_nat_k': nrm((DEC_BATCH, DEPTH, NAT_HEADS, PAST_LEN, HEAD_DIM)),
        'cache_nat_v': nrm((DEC_BATCH, DEPTH, NAT_HEADS, PAST_LEN, HEAD_DIM)),
        'cache_gqa_k': nrm((DEC_BATCH, DEPTH, GQA_KV_HEADS, PAST_LEN, HEAD_DIM)),
        'cache_gqa_v': nrm((DEC_BATCH, DEPTH, GQA_KV_HEADS, PAST_LEN, HEAD_DIM)),
        'state_ret_fwd': nrm((DEC_BATCH, DEPTH, RET_HEADS, RET_DK, RET_DV), 0.1),
        'state_ret_bwd': nrm((DEC_BATCH, DEPTH, RET_HEADS, RET_DK, RET_DV), 0.1),
        'c': nrm((DEC_BATCH, D_MODEL)),
        'c_ctx': nrm((D_MODEL,)),
        'w_mod': nrm((DEPTH, D_MODEL, 6 * D_MODEL), 0.5 * D_MODEL ** -0.5),
        'b_mod': nrm((DEPTH, 6 * D_MODEL), 0.02),
        'norm1': 1.0 + nrm((DEPTH, D_MODEL), 0.05),
        'norm2': 1.0 + nrm((DEPTH, D_MODEL), 0.05),
        'w_in': nrm((DEPTH, D_MODEL, IN_W), D_MODEL ** -0.5),
        'conv_w': nrm((DEPTH, CONV_K, CONV_W), CONV_K ** -0.5),
        'nat_qn': 1.0 + nrm((DEPTH, HEAD_DIM), 0.05),
        'nat_kn': 1.0 + nrm((DEPTH, HEAD_DIM), 0.05),
        'nat_rpb': nrm((DEPTH, NAT_HEADS, 2 * NAT_KR - 1, 2 * NAT_KC - 1), 0.1),
        'gqa_qn': 1.0 + nrm((DEPTH, HEAD_DIM), 0.05),
        'gqa_kn': 1.0 + nrm((DEPTH, HEAD_DIM), 0.05),
        'ret_decay_fwd': base_logit[None] + nrm((DEPTH, RET_HEADS), 0.1),
        'ret_decay_bwd': base_logit[None] + nrm((DEPTH, RET_HEADS), 0.1),
        'ret_gn': 1.0 + nrm((DEPTH, RET_HEADS * RET_DV), 0.05),
        'w_branch': nrm((DEPTH, N_BRANCH, BRANCH_W, D_MODEL), BRANCH_W ** -0.5),
        'w_out': nrm((DEPTH, D_MODEL, D_MODEL), D_MODEL ** -0.5),
        'w_router': nrm((D_MODEL, N_EXPERTS), D_MODEL ** -0.5),
        'b_router': nrm((N_EXPERTS,), 0.01),
        'w_exp_gate': nrm((DEPTH, N_EXPERTS, D_MODEL, D_EXPERT), D_MODEL ** -0.5),
        'w_exp_up': nrm((DEPTH, N_EXPERTS, D_MODEL, D_EXPERT), D_MODEL ** -0.5),
        'w_exp_down': nrm((DEPTH, N_EXPERTS, D_EXPERT, D_MODEL), D_EXPERT ** -0.5),
    }


def reference(x_prompt, x_sample, cache_nat_k, cache_nat_v, cache_gqa_k, cache_gqa_v, state_ret_fwd,
              state_ret_bwd, c, c_ctx, w_mod, b_mod, norm1, norm2, w_in, conv_w, nat_qn, nat_kn, nat_rpb,
              gqa_qn, gqa_kn, ret_decay_fwd, ret_decay_bwd, ret_gn, w_branch, w_out, w_router, b_router,
              w_exp_gate, w_exp_up, w_exp_down):
    cos, sin = axial_rope_tables(x_sample.shape[1])
    cvec_ctx = c_ctx[None, None, :]
    cvec_lat = c[:, None, :]
    xp = x_prompt
    xs = x_sample
    nat_k_l, nat_v_l, gqa_k_l, gqa_v_l, sf_l, sb_l = [], [], [], [], [], []
    for l in range(DEPTH):
        lp = {'w_mod': w_mod[l], 'b_mod': b_mod[l], 'norm1': norm1[l], 'norm2': norm2[l], 'w_in': w_in[l],
              'conv_w': conv_w[l], 'nat_qn': nat_qn[l], 'nat_kn': nat_kn[l], 'nat_rpb': nat_rpb[l],
              'gqa_qn': gqa_qn[l], 'gqa_kn': gqa_kn[l], 'ret_decay_fwd': ret_decay_fwd[l],
              'ret_decay_bwd': ret_decay_bwd[l], 'ret_gn': ret_gn[l], 'w_branch': w_branch[l],
              'w_out': w_out[l], 'w_exp_gate': w_exp_gate[l], 'w_exp_up': w_exp_up[l],
              'w_exp_down': w_exp_down[l]}
        xp, (nk, nv, gk, gv, s_f, s_b) = context_layer(xp, cvec_ctx, lp, w_router, b_router)
        nat_k_l.append(nk)
        nat_v_l.append(nv)
        gqa_k_l.append(gk)
        gqa_v_l.append(gv)
        sf_l.append(s_f.astype(x_prompt.dtype))
        sb_l.append(s_b.astype(x_prompt.dtype))
        ctx = (cache_nat_k[:, l], cache_nat_v[:, l], cache_gqa_k[:, l], cache_gqa_v[:, l],
               state_ret_fwd[:, l], state_ret_bwd[:, l])
        xs = latent_layer(xs, cvec_lat, ctx, lp, w_router, b_router, cos, sin)
    new_nat_k = jnp.stack(nat_k_l, axis=1)
    new_nat_v = jnp.stack(nat_v_l, axis=1)
    new_gqa_k = jnp.stack(gqa_k_l, axis=1)
    new_gqa_v = jnp.stack(gqa_v_l, axis=1)
    new_ret_fwd = jnp.stack(sf_l, axis=1)
    new_ret_bwd = jnp.stack(sb_l, axis=1)
    return (xp, xs, new_nat_k, new_nat_v, new_gqa_k, new_gqa_v, new_ret_fwd, new_ret_bwd)
```

```python
import functools

import numpy as np
import jax
import jax.numpy as jnp
from jax import lax
from jax.experimental import pallas as pl
from jax.experimental.pallas import tpu as pltpu

F32 = jnp.float32
BF16 = jnp.bfloat16

GRID_W = 64
HEAD_DIM = 128
BRANCH_W = 512
N_BRANCH = 4
CONV_K = 3
NAT_KR = 8
NAT_KC = 16
GQA_GROUP = 2
RET_CHUNK = 128
ROPE_THETA = 10000.0
N_EXPERTS = 16
N_GROUPS = 4
TOP_K = 2
EPS = 1e-6
MASKED = -1e30

COL_U, COL_BG, COL_CG = 0, 512, 1024
COL_NQ, COL_NK, COL_NV = 1536, 2048, 2560
COL_GQ, COL_GK, COL_GV = 3072, 3584, 3840
COL_RQ, COL_RK, COL_RV, COL_RG = 4096, 4608, 5120, 5632
PROJ_W = 6144

V7X_VMEM_LIMIT_BYTES = 56 * 1024 * 1024
SUBLANES = 8
LANES = 128

PROJ_TM, PROJ_TN = 1024, 512
MERGE_TM, MERGE_TK = 512, 256
ROUTER_TM = 512
MOE_ROWS = 256
MOE_TT = 256
CONV_TT = 1024
ATT_TQ, ATT_TK = 256, 512
MOD_TN = 1024


def _cparams(n_axes):
    return pltpu.CompilerParams(dimension_semantics=("arbitrary",) * n_axes,
                                vmem_limit_bytes=V7X_VMEM_LIMIT_BYTES)


def _sigmoid(x):
    return 1.0 / (1.0 + jnp.exp(-x))


def _dot(a, b):
    return jnp.dot(a, b, preferred_element_type=F32)


def _dot_nt(a, b):
    return lax.dot_general(a, b, (((1,), (1,)), ((), ())), preferred_element_type=F32)


def _rms(x, gain):
    return x * lax.rsqrt(jnp.mean(x * x, axis=-1, keepdims=True) + EPS) * gain


def _rows(i, n):
    return pl.ds(pl.multiple_of(i * n, n), n)


def _mod_kernel(c_ref, w_ref, b_ref, o_ref):
    c = c_ref[...]
    a = (c * _sigmoid(c)).astype(BF16)
    o_ref[0] = _dot(a, w_ref[0].astype(BF16)) + b_ref[0]


def _modulation(cvecs, w_mod, b_mod):
    depth, d, n = w_mod.shape
    tn = min(MOD_TN, n)
    return pl.pallas_call(
        _mod_kernel,
        out_shape=jax.ShapeDtypeStruct((depth, cvecs.shape[0], n), F32),
        grid=(depth, n // tn),
        in_specs=[pl.BlockSpec(cvecs.shape, lambda l, j: (0, 0)),
                  pl.BlockSpec((1, d, tn), lambda l, j: (l, 0, j)),
                  pl.BlockSpec((1, 1, tn), lambda l, j: (l, 0, j))],
        out_specs=pl.BlockSpec((1, cvecs.shape[0], tn), lambda l, j: (l, 0, j)),
        compiler_params=_cparams(2),
        name="modulation",
    )(cvecs, w_mod, b_mod.reshape(depth, 1, n))


def _normproj_kernel(x_ref, mod_ref, g_ref, w_ref, o_ref, h_scr, *, gate):
    @pl.when(pl.program_id(1) == 0)
    def _():
        y = _rms(x_ref[...], g_ref[...])
        h = y * (1.0 + mod_ref[0, 1:2, :]) + mod_ref[0, 0:1, :]
        h_scr[...] = h.astype(BF16)

    acc = _dot(h_scr[...], w_ref[...])
    o_ref[...] = _sigmoid(acc) if gate else acc


def _norm_project(x, mods, mod_row0, tiles_per_seq, gain, w, *, gate, name):
    n, d = x.shape
    nout = w.shape[1]
    tm, tn = min(PROJ_TM, n), min(PROJ_TN, nout)
    tps = max(tiles_per_seq // tm, 1) if tiles_per_seq else n // tm
    return pl.pallas_call(
        functools.partial(_normproj_kernel, gate=gate),
        out_shape=jax.ShapeDtypeStruct((n, nout), F32),
        grid=(n // tm, nout // tn),
        in_specs=[pl.BlockSpec((tm, d), lambda i, j: (i, 0)),
                  pl.BlockSpec((1, 6, d), lambda i, j: (mod_row0 + i // tps, 0, 0)),
                  pl.BlockSpec((1, d), lambda i, j: (0, 0)),
                  pl.BlockSpec((d, tn), lambda i, j: (0, j))],
        out_specs=pl.BlockSpec((tm, tn), lambda i, j: (i, j)),
        scratch_shapes=[pltpu.VMEM((tm, d), BF16)],
        compiler_params=_cparams(2),
        name=name,
    )(x, mods, gain.reshape(1, d), w)


def _conv_kernel(u_ref, bg_ref, cg_ref, up_ref, cp_ref, un_ref, cn_ref, w_ref, o_ref, *, tt, nt):
    t = pl.program_id(1)
    z = cg_ref[...] * u_ref[...]
    last = SUBLANES - 1
    z_before = jnp.where(t > 0, cp_ref[last:last + 1, :] * up_ref[last:last + 1, :], 0.0)
    z_after = jnp.where(t < nt - 1, cn_ref[0:1, :] * un_ref[0:1, :], 0.0)
    ri = lax.broadcasted_iota(jnp.int32, z.shape, 0)
    z_prev = jnp.where(ri == 0, z_before, pltpu.roll(z, 1, 0))
    z_next = jnp.where(ri == tt - 1, z_after, pltpu.roll(z, tt - 1, 0))
    w = w_ref[...]
    o_ref[...] = bg_ref[...] * (w[0:1, :] * z_prev + w[1:2, :] * z + w[2:3, :] * z_next)


def _short_conv(proj, conv_w, seq_len):
    n = proj.shape[0]
    cw = conv_w.shape[1]
    nseq = n // seq_len
    tt = min(CONV_TT, seq_len)
    nt = seq_len // tt
    cu, cb, cc = COL_U // cw, COL_BG // cw, COL_CG // cw
    nblk8 = n // SUBLANES

    def main(col):
        return pl.BlockSpec((tt, cw), lambda b, t: (b * nt + t, col))

    def before(col):
        return pl.BlockSpec((SUBLANES, cw),
                            lambda b, t: (jnp.maximum((b * nt + t) * (tt // SUBLANES) - 1, 0), col))

    def after(col):
        return pl.BlockSpec((SUBLANES, cw),
                            lambda b, t: (jnp.minimum((b * nt + t + 1) * (tt // SUBLANES), nblk8 - 1), col))

    return pl.pallas_call(
        functools.partial(_conv_kernel, tt=tt, nt=nt),
        out_shape=jax.ShapeDtypeStruct((n, cw), F32),
        grid=(nseq, nt),
        in_specs=[main(cu), main(cb), main(cc), before(cu), before(cc), after(cu), after(cc),
                  pl.BlockSpec(conv_w.shape, lambda b, t: (0, 0))],
        out_specs=pl.BlockSpec((tt, cw), lambda b, t: (b * nt + t, 0)),
        compiler_params=_cparams(2),
        name="short_conv",
    )(proj, proj, proj, proj, proj, proj, proj, conv_w)


def _attn_kernel(*refs, group, seq, tq, tk, rope, cache_len, emit_kv):
    it = iter(refs)
    q_ref, k_ref, v_ref, qn_ref, kn_ref = (next(it) for _ in range(5))
    if rope:
        cos_ref, sin_lo_ref, sin_hi_ref = (next(it) for _ in range(3))
    if cache_len:
        kc_ref, vc_ref = next(it), next(it)
    o_ref = next(it)
    if emit_kv:
        ko_ref, vo_ref = next(it), next(it)
    kb, vb, m_scr, l_scr, acc = (next(it) for _ in range(5))
    scale = HEAD_DIM ** -0.5

    def prep(x, gain_ref, rows):
        y = _rms(x, gain_ref[...])
        if rope:
            y = (y * cos_ref[rows, :] + pltpu.roll(y, HEAD_DIM - HEAD_DIM // 4, 1) * sin_lo_ref[rows, :]
                 + pltpu.roll(y, HEAD_DIM // 4, 1) * sin_hi_ref[rows, :])
        return y

    def key_tile(kt, c):
        rows = _rows(kt, tk)
        kn = prep(k_ref[rows, :], kn_ref, rows)
        v = v_ref[rows, :]
        kb[rows, :] = kn.astype(BF16)
        vb[rows, :] = v.astype(BF16)
        if emit_kv:
            ko_ref[0, 0, rows, :] = kn
            vo_ref[0, 0, rows, :] = v
        return c

    lax.fori_loop(0, seq // tk, key_tile, 0)

    def softmax_step(q, kblk, vblk):
        s = _dot_nt(q, kblk)
        m_prev = m_scr[...]
        m_new = jnp.maximum(m_prev, jnp.max(s, axis=-1, keepdims=True))
        alpha = jnp.exp(m_prev - m_new)
        p = jnp.exp(s - m_new)
        l_scr[...] = alpha * l_scr[...] + jnp.sum(p, axis=-1, keepdims=True)
        acc[...] = alpha * acc[...] + _dot(p.astype(BF16), vblk)
        m_scr[...] = m_new

    def query_tile(qt, c):
        rows = _rows(qt, tq)
        qs = [(prep(q_ref[rows, g * HEAD_DIM:(g + 1) * HEAD_DIM], qn_ref, rows) * scale).astype(BF16)
              for g in range(group)]
        q = qs[0] if group == 1 else jnp.concatenate(qs, axis=0)
        m_scr[...] = jnp.full(m_scr.shape, MASKED, F32)
        l_scr[...] = jnp.zeros(l_scr.shape, F32)
        acc[...] = jnp.zeros(acc.shape, F32)

        def kv_tile(kt, c2):
            r2 = _rows(kt, tk)
            softmax_step(q, kb[r2, :], vb[r2, :])
            return c2

        lax.fori_loop(0, seq // tk, kv_tile, 0)
        if cache_len:
            softmax_step(q, kc_ref[0, 0, 0].astype(BF16), vc_ref[0, 0, 0].astype(BF16))
        o = acc[...] / l_scr[...]
        for g in range(group):
            o_ref[rows, g * HEAD_DIM:(g + 1) * HEAD_DIM] = o[g * tq:(g + 1) * tq]
        return c

    lax.fori_loop(0, seq // tq, query_tile, 0)


def _attention(proj, seq_len, n_kv, group, q_col, k_col, v_col, q_gain, k_gain, *,
               rope_tabs=None, cache=None, emit_kv=False, name):
    n = proj.shape[0]
    nseq = n // seq_len
    hd = HEAD_DIM
    tq, tk = min(ATT_TQ, seq_len), min(ATT_TK, seq_len)
    gw = group * hd
    in_specs = [pl.BlockSpec((seq_len, gw), lambda b, h: (b, q_col // gw + h)),
                pl.BlockSpec((seq_len, hd), lambda b, h: (b, k_col // hd + h)),
                pl.BlockSpec((seq_len, hd), lambda b, h: (b, v_col // hd + h)),
                pl.BlockSpec((1, hd), lambda b, h: (0, 0)),
                pl.BlockSpec((1, hd), lambda b, h: (0, 0))]
    args = [proj, proj, proj, q_gain.reshape(1, hd), k_gain.reshape(1, hd)]
    if rope_tabs is not None:
        in_specs += [pl.BlockSpec((seq_len, hd), lambda b, h: (0, 0))] * 3
        args += list(rope_tabs)
    cache_len = 0
    if cache is not None:
        kc, vc, layer = cache
        cache_len = kc.shape[3]
        in_specs += [pl.BlockSpec((1, 1, 1, cache_len, hd), lambda b, h: (b, layer, h, 0, 0))] * 2
        args += [kc, vc]
    out_shape = [jax.ShapeDtypeStruct((n, n_kv * gw), F32)]
    out_specs = [pl.BlockSpec((seq_len, gw), lambda b, h: (b, h))]
    if emit_kv:
        out_shape += [jax.ShapeDtypeStruct((nseq, n_kv, seq_len, hd), F32)] * 2
        out_specs += [pl.BlockSpec((1, 1, seq_len, hd), lambda b, h: (b, h, 0, 0))] * 2
    m = group * tq
    res = pl.pallas_call(
        functools.partial(_attn_kernel, group=group, seq=seq_len, tq=tq, tk=tk,
                          rope=rope_tabs is not None, cache_len=cache_len, emit_kv=emit_kv),
        out_shape=out_shape,
        grid=(nseq, n_kv),
        in_specs=in_specs,
        out_specs=out_specs,
        scratch_shapes=[pltpu.VMEM((seq_len, hd), BF16), pltpu.VMEM((seq_len, hd), BF16),
                        pltpu.VMEM((m, 1), F32), pltpu.VMEM((m, 1), F32), pltpu.VMEM((m, hd), F32)],
        compiler_params=_cparams(2),
        name=name,
    )(*args)
    return res if emit_kv else res[0]


def _natten_kernel(q_ref, k_ref, v_ref, qn_ref, kn_ref, kc_ref, vc_ref, bias_ref, o_ref, kb, vb, *,
                   seq, width, kr, chunk):
    n_rows = seq // width
    scale = HEAD_DIM ** -0.5

    def key_chunk(i, c):
        rows = _rows(i, chunk)
        kb[rows, :] = _rms(k_ref[rows, :], kn_ref[...]).astype(BF16)
        vb[rows, :] = v_ref[rows, :].astype(BF16)
        return c

    lax.fori_loop(0, seq // chunk, key_chunk, 0)

    def grid_row(r, c):
        r0 = jnp.clip(r - kr // 2, 0, n_rows - kr)
        qrows = _rows(r, width)
        krows = pl.ds(pl.multiple_of(r0 * width, width), kr * width)
        q = (_rms(q_ref[qrows, :], qn_ref[...]) * scale).astype(BF16)
        s_loc = _dot_nt(q, kb[krows, :]) + bias_ref[0, r - r0]
        s_ctx = _dot_nt(q, kc_ref[0, 0, 0].astype(BF16))
        m = jnp.maximum(jnp.max(s_loc, axis=-1, keepdims=True), jnp.max(s_ctx, axis=-1, keepdims=True))
        p_loc = jnp.exp(s_loc - m)
        p_ctx = jnp.exp(s_ctx - m)
        denom = jnp.sum(p_loc, axis=-1, keepdims=True) + jnp.sum(p_ctx, axis=-1, keepdims=True)
        o = _dot(p_loc.astype(BF16), vb[krows, :]) + _dot(p_ctx.astype(BF16), vc_ref[0, 0, 0].astype(BF16))
        o_ref[qrows, :] = o / denom
        return c

    lax.fori_loop(0, n_rows, grid_row, 0)


def _natten_bias(rpb, width, kr):
    kc = NAT_KC
    cols = np.arange(width)
    c0 = np.clip(cols - kc // 2, 0, width - kc)
    j = np.arange(width)
    in_win = (j[None, :] >= c0[:, None]) & (j[None, :] < c0[:, None] + kc)
    ci = np.clip(j[None, :] - cols[:, None] + kc - 1, 0, 2 * kc - 2)
    off = np.arange(kr)
    ri = np.arange(kr)[None, :] - off[:, None] + NAT_KR - 1
    tab = rpb[:, ri[:, None, :, None], ci[None, :, None, :]]
    tab = jnp.where(in_win[None, None, :, None, :], tab, MASKED)
    return tab.reshape(rpb.shape[0], kr, width, kr * width).astype(F32)


def _natten(proj, seq_len, q_gain, k_gain, rpb, kc, vc, layer):
    n = proj.shape[0]
    nseq = n // seq_len
    hd = HEAD_DIM
    n_heads = rpb.shape[0]
    width = GRID_W
    kr = min(NAT_KR, seq_len // width)
    bias = _natten_bias(rpb, width, kr)
    cache_len = kc.shape[3]
    chunk = min(512, seq_len)
    col = lambda c0: pl.BlockSpec((seq_len, hd), lambda b, h: (b, c0 // hd + h))
    return pl.pallas_call(
        functools.partial(_natten_kernel, seq=seq_len, width=width, kr=kr, chunk=chunk),
        out_shape=jax.ShapeDtypeStruct((n, n_heads * hd), F32),
        grid=(nseq, n_heads),
        in_specs=[col(COL_NQ), col(COL_NK), col(COL_NV),
                  pl.BlockSpec((1, hd), lambda b, h: (0, 0)),
                  pl.BlockSpec((1, hd), lambda b, h: (0, 0)),
                  pl.BlockSpec((1, 1, 1, cache_len, hd), lambda b, h: (b, layer, h, 0, 0)),
                  pl.BlockSpec((1, 1, 1, cache_len, hd), lambda b, h: (b, layer, h, 0, 0)),
                  pl.BlockSpec((1, kr, width, kr * width), lambda b, h: (h, 0, 0, 0))],
        out_specs=pl.BlockSpec((seq_len, hd), lambda b, h: (b, h)),
        scratch_shapes=[pltpu.VMEM((seq_len, hd), BF16), pltpu.VMEM((seq_len, hd), BF16)],
        compiler_params=_cparams(2),
        name="natten_latent",
    )(proj, proj, proj, q_gain.reshape(1, hd), k_gain.reshape(1, hd), kc, vc, bias)


def _retention_kernel(q_ref, k_ref, v_ref, rg_ref, lg_ref, gn_ref, sf0_ref, sb0_ref,
                      y_ref, sf_ref, sb_ref, of_scr, ob_scr, *, seq):
    c = RET_CHUNK
    nc = seq // c
    lgf = lg_ref[0, 0:1, :]
    lgb = lg_ref[0, 1:2, :]
    ii = lax.broadcasted_iota(jnp.int32, (c, c), 0)
    jj = lax.broadcasted_iota(jnp.int32, (c, c), 1)
    rel = (ii - jj).astype(F32)
    pos = ii.astype(F32)
    intra_f = jnp.where(rel >= 0, jnp.exp(lgf * jnp.maximum(rel, 0.0)), 0.0)
    intra_b = jnp.where(rel <= 0, jnp.exp(lgb * jnp.maximum(-rel, 0.0)), 0.0)
    qd_f = jnp.exp(lgf * (pos + 1.0))
    kd_f = jnp.exp(lgf * (c - 1.0 - pos))
    cd_f = jnp.exp(lgf * c)
    qd_b = jnp.exp(lgb * (c - pos))
    kd_b = jnp.exp(lgb * pos)
    cd_b = jnp.exp(lgb * c)
    kscale = HEAD_DIM ** -0.5

    def chunk_step(rows, s, intra, qd, kd, cd, o_scr):
        q = q_ref[rows, :]
        k = k_ref[rows, :] * kscale
        v = v_ref[rows, :].astype(BF16)
        att = _dot_nt(q.astype(BF16), k.astype(BF16)) * intra
        o_scr[rows, :] = _dot(att.astype(BF16), v) + _dot((q * qd).astype(BF16), s.astype(BF16))
        return s * cd + _dot((k * kd).T.astype(BF16), v)

    def step(ci, carry):
        sf, sb = carry
        sf = chunk_step(_rows(ci, c), sf, intra_f, qd_f, kd_f, cd_f, of_scr)
        sb = chunk_step(_rows(nc - 1 - ci, c), sb, intra_b, qd_b, kd_b, cd_b, ob_scr)
        return sf, sb

    sf, sb = lax.fori_loop(0, nc, step, (sf0_ref[0, 0, 0], sb0_ref[0, 0, 0]))
    sf_ref[0, 0] = sf
    sb_ref[0, 0] = sb

    def finish(ci, carry):
        rows = _rows(ci, c)
        o = of_scr[rows, :] + ob_scr[rows, :]
        o = o * lax.rsqrt(jnp.mean(o * o, axis=-1, keepdims=True) + EPS)
        g = rg_ref[rows, :]
        y_ref[rows, :] = (g * _sigmoid(g)) * (o * gn_ref[...])
        return carry

    lax.fori_loop(0, nc, finish, 0)


def _retention(proj, seq_len, log_gamma, ret_gn, s_fwd, s_bwd, layer):
    n = proj.shape[0]
    nseq = n // seq_len
    hd = HEAD_DIM
    n_heads = log_gamma.shape[0]
    shared = s_fwd.shape[0] == 1
    col = lambda c0: pl.BlockSpec((seq_len, hd), lambda b, h: (b, c0 // hd + h))
    state = pl.BlockSpec((1, 1, 1, hd, hd), lambda b, h: (0 if shared else b, layer, 0 if shared else h, 0, 0))
    return pl.pallas_call(
        functools.partial(_retention_kernel, seq=seq_len),
        out_shape=[jax.ShapeDtypeStruct((n, n_heads * hd), F32),
                   jax.ShapeDtypeStruct((nseq, n_heads, hd, hd), F32),
                   jax.ShapeDtypeStruct((nseq, n_heads, hd, hd), F32)],
        grid=(nseq, n_heads),
        in_specs=[col(COL_RQ), col(COL_RK), col(COL_RV), col(COL_RG),
                  pl.BlockSpec((1, 2, hd), lambda b, h: (h, 0, 0)),
                  pl.BlockSpec((1, hd), lambda b, h: (0, h)),
                  state, state],
        out_specs=[pl.BlockSpec((seq_len, hd), lambda b, h: (b, h)),
                   pl.BlockSpec((1, 1, hd, hd), lambda b, h: (b, h, 0, 0)),
                   pl.BlockSpec((1, 1, hd, hd), lambda b, h: (b, h, 0, 0))],
        scratch_shapes=[pltpu.VMEM((seq_len, hd), F32), pltpu.VMEM((seq_len, hd), F32)],
        compiler_params=_cparams(2),
        name="retention",
    )(proj, proj, proj, proj, log_gamma, ret_gn.reshape(1, n_heads * hd), s_fwd, s_bwd)


def _merge_kernel(y0_ref, y1_ref, y2_ref, y3_ref, g0_ref, g1_ref, g2_ref, g3_ref, wb_ref, wo_ref,
                  x_ref, mod_ref, o_ref, yb_scr, acc):
    k = pl.program_id(1)

    @pl.when(k == 0)
    def _():
        for i, y_ref in enumerate((y0_ref, y1_ref, y2_ref, y3_ref)):
            yb_scr[i] = y_ref[...].astype(BF16)
        acc[...] = jnp.zeros(acc.shape, F32)

    m = g0_ref[...] * _dot(yb_scr[0], wb_ref[0])
    for i, g_ref in enumerate((g1_ref, g2_ref, g3_ref), start=1):
        m = m + g_ref[...] * _dot(yb_scr[i], wb_ref[i])
    acc[...] += _dot(m.astype(BF16), wo_ref[...])

    @pl.when(k == pl.num_programs(1) - 1)
    def _():
        o_ref[...] = x_ref[...] + mod_ref[0, 2:3, :] * acc[...]


def _merge(x, branches, gates, w_branch, w_out, mods, mod_row0, tiles_per_seq):
    n, d = x.shape
    bw = w_branch.shape[1]
    tm, tk = min(MERGE_TM, n), min(MERGE_TK, d)
    nk = d // tk
    tps = max(tiles_per_seq // tm, 1) if tiles_per_seq else n // tm
    gate = lambda i: pl.BlockSpec((tm, tk), lambda t, k: (t, i * nk + k))
    return pl.pallas_call(
        _merge_kernel,
        out_shape=jax.ShapeDtypeStruct((n, d), F32),
        grid=(n // tm, nk),
        in_specs=[pl.BlockSpec((tm, bw), lambda t, k: (t, 0))] * 4
                 + [gate(0), gate(1), gate(2), gate(3),
                    pl.BlockSpec((N_BRANCH, bw, tk), lambda t, k: (0, 0, k)),
                    pl.BlockSpec((tk, d), lambda t, k: (k, 0)),
                    pl.BlockSpec((tm, d), lambda t, k: (t, 0)),
                    pl.BlockSpec((1, 6, d), lambda t, k: (mod_row0 + t // tps, 0, 0))],
        out_specs=pl.BlockSpec((tm, d), lambda t, k: (t, 0)),
        scratch_shapes=[pltpu.VMEM((N_BRANCH, tm, bw), BF16), pltpu.VMEM((tm, d), F32)],
        compiler_params=_cparams(2),
        name="merge_out",
    )(*branches, gates, gates, gates, gates, w_branch, w_out, x, mods)


def _router_kernel(x_ref, mod_ref, g_ref, wt_ref, b_ref, h_ref, eid_ref, wts_ref):
    y = _rms(x_ref[...], g_ref[...])
    h = y * (1.0 + mod_ref[0, 4:5, :]) + mod_ref[0, 3:4, :]
    h_ref[...] = h
    h_hi = h.astype(BF16)
    h_lo = (h - h_hi.astype(F32)).astype(BF16)
    w = wt_ref[...]
    w_hi = w.astype(BF16)
    w_lo = (w - w_hi.astype(F32)).astype(BF16)
    logits = _dot_nt(w_hi, h_hi) + (_dot_nt(w_lo, h_hi) + _dot_nt(w_hi, h_lo))
    score = _sigmoid(logits)
    sel = score + b_ref[...]
    epg = N_EXPERTS // N_GROUPS
    s = [sel[e:e + 1, :] for e in range(N_EXPERTS)]
    sc = [score[e:e + 1, :] for e in range(N_EXPERTS)]

    def group_score(vals):
        best = None
        for a in range(len(vals)):
            for b in range(a + 1, len(vals)):
                pair = vals[a] + vals[b]
                best = pair if best is None else jnp.maximum(best, pair)
        return best

    gs = [group_score(s[g * epg:(g + 1) * epg]) for g in range(N_GROUPS)]
    g_best = jnp.zeros(gs[0].shape, jnp.int32)
    best = gs[0]
    for g in range(1, N_GROUPS):
        better = gs[g] > best
        g_best = jnp.where(better, g, g_best)
        best = jnp.where(better, gs[g], best)
    in_sel, in_score = [], []
    for k in range(epg):
        v, w_ = s[k], sc[k]
        for g in range(1, N_GROUPS):
            v = jnp.where(g_best == g, s[g * epg + k], v)
            w_ = jnp.where(g_best == g, sc[g * epg + k], w_)
        in_sel.append(v)
        in_score.append(w_)
    i1 = jnp.zeros(g_best.shape, jnp.int32)
    v1, w1 = in_sel[0], in_score[0]
    for k in range(1, epg):
        better = in_sel[k] > v1
        i1 = jnp.where(better, k, i1)
        v1 = jnp.where(better, in_sel[k], v1)
        w1 = jnp.where(better, in_score[k], w1)
    i2 = jnp.zeros(g_best.shape, jnp.int32)
    v2 = jnp.full(v1.shape, -jnp.inf, F32)
    w2 = jnp.zeros(v1.shape, F32)
    for k in range(epg):
        better = (i1 != k) & (in_sel[k] > v2)
        i2 = jnp.where(better, k, i2)
        v2 = jnp.where(better, in_sel[k], v2)
        w2 = jnp.where(better, in_score[k], w2)
    total = w1 + w2
    eid_ref[...] = jnp.concatenate([g_best * epg + i1, g_best * epg + i2], axis=0)
    wts_ref[...] = jnp.concatenate([w1 / total, w2 / total], axis=0)


def _router(x, mods, mod_row0, tiles_per_seq, gain, w_router_t, b_router):
    n, d = x.shape
    tm = min(ROUTER_TM, n)
    tps = max(tiles_per_seq // tm, 1) if tiles_per_seq else n // tm
    e = w_router_t.shape[0]
    return pl.pallas_call(
        _router_kernel,
        out_shape=[jax.ShapeDtypeStruct((n, d), F32),
                   jax.ShapeDtypeStruct((TOP_K, n), jnp.int32),
                   jax.ShapeDtypeStruct((TOP_K, n), F32)],
        grid=(n // tm,),
        in_specs=[pl.BlockSpec((tm, d), lambda i: (i, 0)),
                  pl.BlockSpec((1, 6, d), lambda i: (mod_row0 + i // tps, 0, 0)),
                  pl.BlockSpec((1, d), lambda i: (0, 0)),
                  pl.BlockSpec((e, d), lambda i: (0, 0)),
                  pl.BlockSpec((e, 1), lambda i: (0, 0))],
        out_specs=[pl.BlockSpec((tm, d), lambda i: (i, 0)),
                   pl.BlockSpec((TOP_K, tm), lambda i: (0, i)),
                   pl.BlockSpec((TOP_K, tm), lambda i: (0, i))],
        compiler_params=_cparams(1),
        name="router",
    )(x, mods, gain.reshape(1, d), w_router_t, b_router.reshape(e, 1))


def _row_copy(src_ref, src_row, dst_ref, dst_row, sem):
    return pltpu.make_async_copy(src_ref.at[pl.ds(src_row, 1)], dst_ref.at[pl.ds(dst_row, 1)], sem)


def _dispatch_kernel(slot_ref, h_ref, buf_in_ref, buf_ref, sem, *, tt):
    del buf_in_ref

    def copies(j):
        return [_row_copy(h_ref, j, buf_ref, slot_ref[0, 0, k * tt + j], sem) for k in range(TOP_K)]

    def start(j, c):
        for cp in copies(j):
            cp.start()
        return c

    def wait(j, c):
        for cp in copies(j):
            cp.wait()
        return c

    lax.fori_loop(0, tt, start, 0)
    lax.fori_loop(0, tt, wait, 0)


def _dispatch(h, slots, buf):
    n, d = h.shape
    tt = min(MOE_TT, n)
    return pl.pallas_call(
        functools.partial(_dispatch_kernel, tt=tt),
        out_shape=jax.ShapeDtypeStruct(buf.shape, buf.dtype),
        grid=(n // tt,),
        in_specs=[pl.BlockSpec((1, 1, TOP_K * tt), lambda i: (i, 0, 0), memory_space=pltpu.SMEM),
                  pl.BlockSpec((tt, d), lambda i: (i, 0)),
                  pl.BlockSpec(memory_space=pl.ANY)],
        out_specs=pl.BlockSpec(memory_space=pl.ANY),
        scratch_shapes=[pltpu.SemaphoreType.DMA],
        input_output_aliases={2: 0},
        compiler_params=_cparams(1),
        name="moe_dispatch",
    )(slots, h, buf)


def _ffn_kernel(be_ref, nu_ref, x_ref, wg_ref, wu_ref, wd_ref, o_ref):
    del be_ref
    live = pl.program_id(0) < nu_ref[0]

    @pl.when(live)
    def _():
        x = x_ref[...].astype(BF16)
        a = _dot(x, wg_ref[0])
        b = _dot(x, wu_ref[0])
        o_ref[...] = _dot(((a * _sigmoid(a)) * b).astype(BF16), wd_ref[0])

    @pl.when(jnp.logical_not(live))
    def _():
        o_ref[...] = jnp.zeros(o_ref.shape, F32)


def _expert_ffn(buf, block_expert, n_used, w_gate, w_up, w_down):
    rows, d = buf.shape
    de = w_gate.shape[2]
    rb = MOE_ROWS
    grid_spec = pltpu.PrefetchScalarGridSpec(
        num_scalar_prefetch=2,
        grid=(rows // rb,),
        in_specs=[pl.BlockSpec((rb, d), lambda i, be, nu: (i, 0)),
                  pl.BlockSpec((1, d, de), lambda i, be, nu: (be[i], 0, 0)),
                  pl.BlockSpec((1, d, de), lambda i, be, nu: (be[i], 0, 0)),
                  pl.BlockSpec((1, de, d), lambda i, be, nu: (be[i], 0, 0))],
        out_specs=pl.BlockSpec((rb, d), lambda i, be, nu: (i, 0)))
    return pl.pallas_call(
        _ffn_kernel,
        out_shape=jax.ShapeDtypeStruct((rows, d), F32),
        grid_spec=grid_spec,
        compiler_params=_cparams(1),
        name="expert_ffn",
    )(block_expert, n_used, buf, w_gate, w_up, w_down)


def _combine_kernel(slot_ref, yb_ref, x_ref, w_ref, mod_ref, o_ref, g0, g1, sem, *, tt):
    bufs = (g0, g1)

    def copies(j):
        return [_row_copy(yb_ref, slot_ref[0, 0, k * tt + j], bufs[k], j, sem) for k in range(TOP_K)]

    def start(j, c):
        for cp in copies(j):
            cp.start()
        return c

    def wait(j, c):
        for cp in copies(j):
            cp.wait()
        return c

    lax.fori_loop(0, tt, start, 0)
    lax.fori_loop(0, tt, wait, 0)
    w = w_ref[...]
    y = w[:, 0:1] * g0[...] + w[:, 1:2] * g1[...]
    o_ref[...] = x_ref[...] + mod_ref[0, 5:6, :] * y


def _combine(x, yb, slots, wts, mods, mod_row0, tiles_per_seq):
    n, d = x.shape
    tt = min(MOE_TT, n)
    tps = max(tiles_per_seq // tt, 1) if tiles_per_seq else n // tt
    return pl.pallas_call(
        functools.partial(_combine_kernel, tt=tt),
        out_shape=jax.ShapeDtypeStruct((n, d), F32),
        grid=(n // tt,),
        in_specs=[pl.BlockSpec((1, 1, TOP_K * tt), lambda i: (i, 0, 0), memory_space=pltpu.SMEM),
                  pl.BlockSpec(memory_space=pl.ANY),
                  pl.BlockSpec((tt, d), lambda i: (i, 0)),
                  pl.BlockSpec((tt, TOP_K), lambda i: (i, 0)),
                  pl.BlockSpec((1, 6, d), lambda i: (mod_row0 + i // tps, 0, 0))],
        out_specs=pl.BlockSpec((tt, d), lambda i: (i, 0)),
        scratch_shapes=[pltpu.VMEM((tt, d), F32), pltpu.VMEM((tt, d), F32), pltpu.SemaphoreType.DMA],
        compiler_params=_cparams(1),
        name="moe_combine",
    )(slots, yb, x, wts, mods)


def _slot_blocks(slots, tt):
    k, n = slots.shape
    return slots.reshape(k, n // tt, tt).transpose(1, 0, 2).reshape(n // tt, 1, k * tt)


def _moe(xs, mods, mod_rows, seq_lens, gain, w_router, b_router, w_gate, w_up, w_down):
    d = xs[0].shape[1]
    w_router_t = w_router.T
    routed = [_router(x, mods, r0, sl, gain, w_router_t, b_router) for x, r0, sl in zip(xs, mod_rows, seq_lens)]
    flat_e = jnp.concatenate([r[1].reshape(-1) for r in routed])
    n_assign = flat_e.shape[0]
    onehot = (flat_e[:, None] == jnp.arange(N_EXPERTS, dtype=jnp.int32)[None, :]).astype(jnp.int32)
    csum = jnp.cumsum(onehot, axis=0)
    rank = jnp.take_along_axis(csum, flat_e[:, None], axis=1)[:, 0] - 1
    counts = csum[-1]
    padded = (counts + MOE_ROWS - 1) // MOE_ROWS * MOE_ROWS
    pend = jnp.cumsum(padded)
    slot = (pend - padded)[flat_e] + rank
    n_blocks = -(-n_assign // MOE_ROWS) + N_EXPERTS
    block_expert = jnp.minimum(
        jnp.searchsorted(pend, jnp.arange(n_blocks, dtype=jnp.int32) * MOE_ROWS, side='right'),
        N_EXPERTS - 1).astype(jnp.int32)
    n_used = (pend[-1:] // MOE_ROWS).astype(jnp.int32)
    buf = jnp.zeros((n_blocks * MOE_ROWS, d), F32)
    slot_blocks, off = [], 0
    for x, r in zip(xs, routed):
        n = x.shape[0]
        sb = _slot_blocks(slot[off:off + TOP_K * n].reshape(TOP_K, n), min(MOE_TT, n))
        off += TOP_K * n
        slot_blocks.append(sb)
        buf = _dispatch(r[0], sb, buf)
    yb = _expert_ffn(buf, block_expert, n_used, w_gate, w_up, w_down)
    return [_combine(x, yb, sb, r[2].T, mods, r0, sl)
            for x, r, sb, r0, sl in zip(xs, routed, slot_blocks, mod_rows, seq_lens)]


def _rope_tables(n_tok):
    t = jnp.arange(n_tok)
    row = (t // GRID_W).astype(F32)
    col = (t % GRID_W).astype(F32)
    quarter = HEAD_DIM // 4
    inv = ROPE_THETA ** (-jnp.arange(quarter, dtype=F32) / quarter)
    ar = row[:, None] * inv
    ac = col[:, None] * inv
    ang = jnp.concatenate([ar, ar, ac, ac], axis=-1)
    cos, sin = jnp.cos(ang), jnp.sin(ang)
    first = (jnp.arange(HEAD_DIM) % (2 * quarter)) < quarter
    return cos, jnp.where(first, -sin, 0.0), jnp.where(first, 0.0, sin)


def kernel(x_prompt, x_sample, cache_nat_k, cache_nat_v, cache_gqa_k, cache_gqa_v, state_ret_fwd,
           state_ret_bwd, c, c_ctx, w_mod, b_mod, norm1, norm2, w_in, conv_w, nat_qn, nat_kn, nat_rpb,
           gqa_qn, gqa_kn, ret_decay_fwd, ret_decay_bwd, ret_gn, w_branch, w_out, w_router, b_router,
           w_exp_gate, w_exp_up, w_exp_down):
    batch, seq, d = x_prompt.shape
    dec_batch, dec_seq, _ = x_sample.shape
    depth = w_mod.shape[0]
    n_heads = nat_rpb.shape[1]
    n_kv = cache_gqa_k.shape[2]
    hd = HEAD_DIM

    xp = x_prompt.reshape(batch * seq, d)
    xs = x_sample.reshape(dec_batch * dec_seq, d)
    cvecs = jnp.zeros((SUBLANES, d), F32).at[0].set(c_ctx).at[1:1 + dec_batch].set(c)
    mods_all = _modulation(cvecs, w_mod, b_mod).reshape(depth, SUBLANES, 6, d)
    rope_tabs = _rope_tables(dec_seq)
    zero_state = jnp.zeros((1, depth, 1, hd, hd), F32)

    caches = [[] for _ in range(6)]
    for l in range(depth):
        mods = mods_all[l]
        w_proj = w_in[l][:, :PROJ_W].astype(BF16)
        w_gatel = w_in[l][:, PROJ_W:].astype(BF16)
        wb = w_branch[l].astype(BF16)
        wo = w_out[l].astype(BF16)
        lg = jnp.stack([jax.nn.log_sigmoid(ret_decay_fwd[l].astype(F32)),
                        jax.nn.log_sigmoid(ret_decay_bwd[l].astype(F32))], axis=1)
        lg = jnp.broadcast_to(lg[:, :, None], (lg.shape[0], 2, hd))

        pc = _norm_project(xp, mods, 0, 0, norm1[l], w_proj, gate=False, name="in_proj_ctx")
        gc = _norm_project(xp, mods, 0, 0, norm1[l], w_gatel, gate=True, name="gate_proj_ctx")
        conv_c = _short_conv(pc, conv_w[l], seq)
        nat_c, nk, nv = _attention(pc, seq, n_heads, 1, COL_NQ, COL_NK, COL_NV, nat_qn[l], nat_kn[l],
                                   emit_kv=True, name="nat_ctx")
        gqa_c, gk, gv = _attention(pc, seq, n_kv, GQA_GROUP, COL_GQ, COL_GK, COL_GV, gqa_qn[l], gqa_kn[l],
                                   emit_kv=True, name="gqa_ctx")
        ret_c, s_f, s_b = _retention(pc, seq, lg, ret_gn[l], zero_state, zero_state, l)
        for lst, val in zip(caches, (nk, nv, gk, gv, s_f, s_b)):
            lst.append(val)
        xp1 = _merge(xp, (conv_c, nat_c, gqa_c, ret_c), gc, wb, wo, mods, 0, 0)

        pl_ = _norm_project(xs, mods, 1, dec_seq, norm1[l], w_proj, gate=False, name="in_proj_lat")
        gl = _norm_project(xs, mods, 1, dec_seq, norm1[l], w_gatel, gate=True, name="gate_proj_lat")
        conv_l = _short_conv(pl_, conv_w[l], dec_seq)
        nat_l = _natten(pl_, dec_seq, nat_qn[l], nat_kn[l], nat_rpb[l], cache_nat_k, cache_nat_v, l)
        gqa_l = _attention(pl_, dec_seq, n_kv, GQA_GROUP, COL_GQ, COL_GK, COL_GV, gqa_qn[l], gqa_kn[l],
                           rope_tabs=rope_tabs, cache=(cache_gqa_k, cache_gqa_v, l), name="gqa_lat")
        ret_l, _, _ = _retention(pl_, dec_seq, lg, ret_gn[l], state_ret_fwd, state_ret_bwd, l)
        xs1 = _merge(xs, (conv_l, nat_l, gqa_l, ret_l), gl, wb, wo, mods, 1, dec_seq)

        xp, xs = _moe([xp1, xs1], mods, [0, 1], [0, dec_seq], norm2[l], w_router, b_router,
                      w_exp_gate[l].astype(BF16), w_exp_up[l].astype(BF16), w_exp_down[l].astype(BF16))

    outs = [jnp.stack(v, axis=1) for v in caches]
    return (xp.reshape(batch, seq, d), xs.reshape(dec_batch, dec_seq, d), *outs)
```

```python
import functools

import numpy as np
import jax
import jax.numpy as jnp
from jax import lax
from jax.experimental import pallas as pl
from jax.experimental.pallas import tpu as pltpu

F32 = jnp.float32
BF16 = jnp.bfloat16

GRID_W = 64
HEAD_DIM = 128
BRANCH_W = 512
N_BRANCH = 4
CONV_K = 3
NAT_KR = 8
NAT_KC = 16
GQA_GROUP = 2
RET_CHUNK = 128
ROPE_THETA = 10000.0
N_EXPERTS = 16
N_GROUPS = 4
TOP_K = 2
EPS = 1e-6
MASKED = -1e30

COL_U, COL_BG, COL_CG = 0, 512, 1024
COL_NQ, COL_NK, COL_NV = 1536, 2048, 2560
COL_GQ, COL_GK, COL_GV = 3072, 3584, 3840
COL_RQ, COL_RK, COL_RV, COL_RG = 4096, 4608, 5120, 5632
PROJ_W = 6144

V7X_VMEM_LIMIT_BYTES = 56 * 1024 * 1024
SUBLANES = 8
LANES = 128

PROJ_TM, PROJ_TN = 1024, 512
MERGE_TM, MERGE_TK = 512, 256
ROUTER_TM = 512
MOE_ROWS = 256
MOE_TT = 256
CONV_TT = 1024
ATT_TQ, ATT_TK = 512, 1024
MOD_TN = 1024
NAT_ROWS = 8


def _cparams(n_axes):
    return pltpu.CompilerParams(dimension_semantics=("arbitrary",) * n_axes,
                                vmem_limit_bytes=V7X_VMEM_LIMIT_BYTES)


def _sigmoid(x):
    return 1.0 / (1.0 + jnp.exp(-x))


def _dot(a, b):
    return jnp.dot(a, b, preferred_element_type=F32)


def _dot_nt(a, b):
    return lax.dot_general(a, b, (((1,), (1,)), ((), ())), preferred_element_type=F32)


def _rms(x, gain):
    return x * lax.rsqrt(jnp.mean(x * x, axis=-1, keepdims=True) + EPS) * gain


def _rows(i, n):
    return pl.ds(pl.multiple_of(i * n, n), n)


def _mod_kernel(c_ref, w_ref, b_ref, o_ref):
    c = c_ref[...]
    a = (c * _sigmoid(c)).astype(BF16)
    o_ref[0] = _dot(a, w_ref[0].astype(BF16)) + b_ref[0]


def _modulation(cvecs, w_mod, b_mod):
    depth, d, n = w_mod.shape
    tn = min(MOD_TN, n)
    return pl.pallas_call(
        _mod_kernel,
        out_shape=jax.ShapeDtypeStruct((depth, cvecs.shape[0], n), F32),
        grid=(depth, n // tn),
        in_specs=[pl.BlockSpec(cvecs.shape, lambda l, j: (0, 0)),
                  pl.BlockSpec((1, d, tn), lambda l, j: (l, 0, j)),
                  pl.BlockSpec((1, 1, tn), lambda l, j: (l, 0, j))],
        out_specs=pl.BlockSpec((1, cvecs.shape[0], tn), lambda l, j: (l, 0, j)),
        compiler_params=_cparams(2),
        name="modulation",
    )(cvecs, w_mod, b_mod.reshape(depth, 1, n))


def _normproj_kernel(x_ref, mod_ref, g_ref, w_ref, o_ref, h_scr, *, gate):
    @pl.when(pl.program_id(1) == 0)
    def _():
        y = _rms(x_ref[...], g_ref[...])
        h = y * (1.0 + mod_ref[0, 1:2, :]) + mod_ref[0, 0:1, :]
        h_scr[...] = h.astype(BF16)

    acc = _dot(h_scr[...], w_ref[...])
    o_ref[...] = _sigmoid(acc) if gate else acc


def _norm_project(x, mods, mod_row0, tiles_per_seq, gain, w, *, gate, name):
    n, d = x.shape
    nout = w.shape[1]
    tm, tn = min(PROJ_TM, n), min(PROJ_TN, nout)
    tps = max(tiles_per_seq // tm, 1) if tiles_per_seq else n // tm
    return pl.pallas_call(
        functools.partial(_normproj_kernel, gate=gate),
        out_shape=jax.ShapeDtypeStruct((n, nout), F32),
        grid=(n // tm, nout // tn),
        in_specs=[pl.BlockSpec((tm, d), lambda i, j: (i, 0)),
                  pl.BlockSpec((1, 6, d), lambda i, j: (mod_row0 + i // tps, 0, 0)),
                  pl.BlockSpec((1, d), lambda i, j: (0, 0)),
                  pl.BlockSpec((d, tn), lambda i, j: (0, j))],
        out_specs=pl.BlockSpec((tm, tn), lambda i, j: (i, j)),
        scratch_shapes=[pltpu.VMEM((tm, d), BF16)],
        compiler_params=_cparams(2),
        name=name,
    )(x, mods, gain.reshape(1, d), w)


def _conv_kernel(u_ref, bg_ref, cg_ref, up_ref, cp_ref, un_ref, cn_ref, w_ref, o_ref, *, tt, nt):
    t = pl.program_id(1)
    z = cg_ref[...] * u_ref[...]
    last = SUBLANES - 1
    z_before = jnp.where(t > 0, cp_ref[last:last + 1, :] * up_ref[last:last + 1, :], 0.0)
    z_after = jnp.where(t < nt - 1, cn_ref[0:1, :] * un_ref[0:1, :], 0.0)
    ri = lax.broadcasted_iota(jnp.int32, z.shape, 0)
    z_prev = jnp.where(ri == 0, z_before, pltpu.roll(z, 1, 0))
    z_next = jnp.where(ri == tt - 1, z_after, pltpu.roll(z, tt - 1, 0))
    w = w_ref[...]
    o_ref[...] = bg_ref[...] * (w[0:1, :] * z_prev + w[1:2, :] * z + w[2:3, :] * z_next)


def _short_conv(proj, conv_w, seq_len):
    n = proj.shape[0]
    cw = conv_w.shape[1]
    nseq = n // seq_len
    tt = min(CONV_TT, seq_len)
    nt = seq_len // tt
    cu, cb, cc = COL_U // cw, COL_BG // cw, COL_CG // cw
    nblk8 = n // SUBLANES

    def main(col):
        return pl.BlockSpec((tt, cw), lambda b, t: (b * nt + t, col))

    def before(col):
        return pl.BlockSpec((SUBLANES, cw),
                            lambda b, t: (jnp.maximum((b * nt + t) * (tt // SUBLANES) - 1, 0), col))

    def after(col):
        return pl.BlockSpec((SUBLANES, cw),
                            lambda b, t: (jnp.minimum((b * nt + t + 1) * (tt // SUBLANES), nblk8 - 1), col))

    return pl.pallas_call(
        functools.partial(_conv_kernel, tt=tt, nt=nt),
        out_shape=jax.ShapeDtypeStruct((n, cw), F32),
        grid=(nseq, nt),
        in_specs=[main(cu), main(cb), main(cc), before(cu), before(cc), after(cu), after(cc),
                  pl.BlockSpec(conv_w.shape, lambda b, t: (0, 0))],
        out_specs=pl.BlockSpec((tt, cw), lambda b, t: (b * nt + t, 0)),
        compiler_params=_cparams(2),
        name="short_conv",
    )(proj, proj, proj, proj, proj, proj, proj, conv_w)


def _attn_kernel(*refs, group, seq, tq, tk, rope, cache_len, emit_kv):
    it = iter(refs)
    q_ref, k_ref, v_ref, qn_ref, kn_ref = (next(it) for _ in range(5))
    if rope:
        cos_ref, sin_lo_ref, sin_hi_ref = (next(it) for _ in range(3))
    if cache_len:
        kc_ref, vc_ref = next(it), next(it)
    o_ref = next(it)
    if emit_kv:
        ko_ref, vo_ref = next(it), next(it)
    kb, vb, m_scr, l_scr, acc = (next(it) for _ in range(5))
    scale = HEAD_DIM ** -0.5
    hd = HEAD_DIM

    def prep(x, gain_ref, rows):
        y = _rms(x, gain_ref[...])
        if rope:
            y = (y * cos_ref[rows, :] + pltpu.roll(y, hd - hd // 4, 1) * sin_lo_ref[rows, :]
                 + pltpu.roll(y, hd // 4, 1) * sin_hi_ref[rows, :])
        return y

    def key_tile(kt, c):
        rows = _rows(kt, tk)
        kn = prep(k_ref[rows, :], kn_ref, rows)
        v = v_ref[rows, :]
        kb[rows, :] = kn.astype(BF16)
        vb[rows, :] = v.astype(BF16)
        if emit_kv:
            ko_ref[0, 0, rows, :] = kn
            vo_ref[0, 0, rows, :] = v
        return c

    lax.fori_loop(0, seq // tk, key_tile, 0)

    def softmax_step(q, kblk, vblk):
        s = _dot_nt(q, kblk)
        m_prev = m_scr[...]
        m_new = jnp.maximum(m_prev, jnp.max(s, axis=-1, keepdims=True))
        alpha = jnp.exp(m_prev - m_new)
        p = jnp.exp(s - jnp.concatenate([m_new] * (s.shape[1] // hd), axis=1))
        l_scr[...] = alpha * l_scr[...] + jnp.sum(p, axis=-1, keepdims=True)
        acc[...] = alpha * acc[...] + _dot(p.astype(BF16), vblk)
        m_scr[...] = m_new

    def query_tile(qt, c):
        rows = _rows(qt, tq)
        qs = [(prep(q_ref[rows, g * hd:(g + 1) * hd], qn_ref, rows) * scale).astype(BF16)
              for g in range(group)]
        q = qs[0] if group == 1 else jnp.concatenate(qs, axis=0)
        m_scr[...] = jnp.full(m_scr.shape, MASKED, F32)
        l_scr[...] = jnp.zeros(l_scr.shape, F32)
        acc[...] = jnp.zeros(acc.shape, F32)

        def kv_tile(kt, c2):
            r2 = _rows(kt, tk)
            softmax_step(q, kb[r2, :], vb[r2, :])
            return c2

        lax.fori_loop(0, seq // tk, kv_tile, 0)
        if cache_len:
            softmax_step(q, kc_ref[0, 0, 0].astype(BF16), vc_ref[0, 0, 0].astype(BF16))
        o = acc[...] / l_scr[...]
        for g in range(group):
            o_ref[rows, g * hd:(g + 1) * hd] = o[g * tq:(g + 1) * tq]
        return c

    lax.fori_loop(0, seq // tq, query_tile, 0)


def _attention(proj, seq_len, n_kv, group, q_col, k_col, v_col, q_gain, k_gain, *,
               rope_tabs=None, cache=None, emit_kv=False, name):
    n = proj.shape[0]
    nseq = n // seq_len
    hd = HEAD_DIM
    tq, tk = min(ATT_TQ, seq_len), min(ATT_TK, seq_len)
    gw = group * hd
    in_specs = [pl.BlockSpec((seq_len, gw), lambda b, h: (b, q_col // gw + h)),
                pl.BlockSpec((seq_len, hd), lambda b, h: (b, k_col // hd + h)),
                pl.BlockSpec((seq_len, hd), lambda b, h: (b, v_col // hd + h)),
                pl.BlockSpec((1, hd), lambda b, h: (0, 0)),
                pl.BlockSpec((1, hd), lambda b, h: (0, 0))]
    args = [proj, proj, proj, q_gain.reshape(1, hd), k_gain.reshape(1, hd)]
    if rope_tabs is not None:
        in_specs += [pl.BlockSpec((seq_len, hd), lambda b, h: (0, 0))] * 3
        args += list(rope_tabs)
    cache_len = 0
    if cache is not None:
        kc, vc, layer = cache
        cache_len = kc.shape[3]
        in_specs += [pl.BlockSpec((1, 1, 1, cache_len, hd), lambda b, h: (b, layer, h, 0, 0))] * 2
        args += [kc, vc]
    out_shape = [jax.ShapeDtypeStruct((n, n_kv * gw), F32)]
    out_specs = [pl.BlockSpec((seq_len, gw), lambda b, h: (b, h))]
    if emit_kv:
        out_shape += [jax.ShapeDtypeStruct((nseq, n_kv, seq_len, hd), F32)] * 2
        out_specs += [pl.BlockSpec((1, 1, seq_len, hd), lambda b, h: (b, h, 0, 0))] * 2
    m = group * tq
    res = pl.pallas_call(
        functools.partial(_attn_kernel, group=group, seq=seq_len, tq=tq, tk=tk,
                          rope=rope_tabs is not None, cache_len=cache_len, emit_kv=emit_kv),
        out_shape=out_shape,
        grid=(nseq, n_kv),
        in_specs=in_specs,
        out_specs=out_specs,
        scratch_shapes=[pltpu.VMEM((seq_len, hd), BF16), pltpu.VMEM((seq_len, hd), BF16),
                        pltpu.VMEM((m, hd), F32), pltpu.VMEM((m, hd), F32), pltpu.VMEM((m, hd), F32)],
        compiler_params=_cparams(2),
        name=name,
    )(*args)
    return res if emit_kv else res[0]


def _natten_kernel(q_ref, k_ref, v_ref, qn_ref, kn_ref, kc_ref, vc_ref, bias_ref, o_ref, kb, vb, *,
                   seq, width, kr, chunk):
    n_rows = seq // width
    scale = HEAD_DIM ** -0.5

    def key_chunk(i, c):
        rows = _rows(i, chunk)
        kb[rows, :] = _rms(k_ref[rows, :], kn_ref[...]).astype(BF16)
        vb[rows, :] = v_ref[rows, :].astype(BF16)
        return c

    lax.fori_loop(0, seq // chunk, key_chunk, 0)

    rg = NAT_ROWS if n_rows % NAT_ROWS == 0 else 1

    def row_group(gi, c):
        qrows = _rows(gi, rg * width)
        q = (_rms(q_ref[qrows, :], qn_ref[...]) * scale).astype(BF16)
        s_ctx = _dot_nt(q, kc_ref[0, 0, 0].astype(BF16))
        krows, s_loc = [], []
        for j in range(rg):
            r = gi * rg + j
            r0 = jnp.clip(r - kr // 2, 0, n_rows - kr)
            krows.append(pl.ds(pl.multiple_of(r0 * width, width), kr * width))
            s_loc.append(_dot_nt(q[j * width:(j + 1) * width], kb[krows[j], :]) + bias_ref[0, r - r0])
        s_loc = jnp.concatenate(s_loc, axis=0) if rg > 1 else s_loc[0]
        m = jnp.maximum(jnp.max(s_loc, axis=-1, keepdims=True), jnp.max(s_ctx, axis=-1, keepdims=True))
        p_loc = jnp.exp(s_loc - m)
        p_ctx = jnp.exp(s_ctx - m)
        denom = jnp.sum(p_loc, axis=-1, keepdims=True) + jnp.sum(p_ctx, axis=-1, keepdims=True)
        p_loc = p_loc.astype(BF16)
        o_loc = [_dot(p_loc[j * width:(j + 1) * width], vb[krows[j], :]) for j in range(rg)]
        o_loc = jnp.concatenate(o_loc, axis=0) if rg > 1 else o_loc[0]
        o = o_loc + _dot(p_ctx.astype(BF16), vc_ref[0, 0, 0].astype(BF16))
        o_ref[qrows, :] = o / denom
        return c

    lax.fori_loop(0, n_rows // rg, row_group, 0)


def _natten_bias(rpb, width, kr):
    kc = NAT_KC
    cols = np.arange(width)
    c0 = np.clip(cols - kc // 2, 0, width - kc)
    j = np.arange(width)
    in_win = (j[None, :] >= c0[:, None]) & (j[None, :] < c0[:, None] + kc)
    ci = np.clip(j[None, :] - cols[:, None] + kc - 1, 0, 2 * kc - 2)
    off = np.arange(kr)
    ri = np.arange(kr)[None, :] - off[:, None] + NAT_KR - 1
    tab = rpb[:, ri[:, None, :, None], ci[None, :, None, :]]
    tab = jnp.where(in_win[None, None, :, None, :], tab, MASKED)
    return tab.reshape(rpb.shape[0], kr, width, kr * width).astype(F32)


def _natten(proj, seq_len, q_gain, k_gain, rpb, kc, vc, layer):
    n = proj.shape[0]
    nseq = n // seq_len
    hd = HEAD_DIM
    n_heads = rpb.shape[0]
    width = GRID_W
    kr = min(NAT_KR, seq_len // width)
    bias = _natten_bias(rpb, width, kr)
    cache_len = kc.shape[3]
    chunk = min(512, seq_len)
    col = lambda c0: pl.BlockSpec((seq_len, hd), lambda b, h: (b, c0 // hd + h))
    return pl.pallas_call(
        functools.partial(_natten_kernel, seq=seq_len, width=width, kr=kr, chunk=chunk),
        out_shape=jax.ShapeDtypeStruct((n, n_heads * hd), F32),
        grid=(nseq, n_heads),
        in_specs=[col(COL_NQ), col(COL_NK), col(COL_NV),
                  pl.BlockSpec((1, hd), lambda b, h: (0, 0)),
                  pl.BlockSpec((1, hd), lambda b, h: (0, 0)),
                  pl.BlockSpec((1, 1, 1, cache_len, hd), lambda b, h: (b, layer, h, 0, 0)),
                  pl.BlockSpec((1, 1, 1, cache_len, hd), lambda b, h: (b, layer, h, 0, 0)),
                  pl.BlockSpec((1, kr, width, kr * width), lambda b, h: (h, 0, 0, 0))],
        out_specs=pl.BlockSpec((seq_len, hd), lambda b, h: (b, h)),
        scratch_shapes=[pltpu.VMEM((seq_len, hd), BF16), pltpu.VMEM((seq_len, hd), BF16)],
        compiler_params=_cparams(2),
        name="natten_latent",
    )(proj, proj, proj, q_gain.reshape(1, hd), k_gain.reshape(1, hd), kc, vc, bias)


def _retention_kernel(q_ref, k_ref, v_ref, rg_ref, lg_ref, gn_ref, sf0_ref, sb0_ref,
                      y_ref, sf_ref, sb_ref, of_scr, ob_scr, *, seq):
    c = RET_CHUNK
    nc = seq // c
    lgf = lg_ref[0, 0:1, :]
    lgb = lg_ref[0, 1:2, :]
    ii = lax.broadcasted_iota(jnp.int32, (c, c), 0)
    jj = lax.broadcasted_iota(jnp.int32, (c, c), 1)
    rel = (ii - jj).astype(F32)
    pos = ii.astype(F32)
    intra_f = jnp.where(rel >= 0, jnp.exp(lgf * jnp.maximum(rel, 0.0)), 0.0)
    intra_b = jnp.where(rel <= 0, jnp.exp(lgb * jnp.maximum(-rel, 0.0)), 0.0)
    qd_f = jnp.exp(lgf * (pos + 1.0))
    kd_f = jnp.exp(lgf * (c - 1.0 - pos))
    cd_f = jnp.exp(lgf * c)
    qd_b = jnp.exp(lgb * (c - pos))
    kd_b = jnp.exp(lgb * pos)
    cd_b = jnp.exp(lgb * c)
    kscale = HEAD_DIM ** -0.5

    def chunk_step(rows, s, intra, qd, kd, cd, o_scr):
        q = q_ref[rows, :]
        k = k_ref[rows, :] * kscale
        v = v_ref[rows, :].astype(BF16)
        att = _dot_nt(q.astype(BF16), k.astype(BF16)) * intra
        o_scr[rows, :] = _dot(att.astype(BF16), v) + _dot((q * qd).astype(BF16), s.astype(BF16))
        return s * cd + _dot((k * kd).T.astype(BF16), v)

    def step(ci, carry):
        sf, sb = carry
        sf = chunk_step(_rows(ci, c), sf, intra_f, qd_f, kd_f, cd_f, of_scr)
        sb = chunk_step(_rows(nc - 1 - ci, c), sb, intra_b, qd_b, kd_b, cd_b, ob_scr)
        return sf, sb

    sf, sb = lax.fori_loop(0, nc, step, (sf0_ref[0, 0, 0], sb0_ref[0, 0, 0]))
    sf_ref[0, 0] = sf
    sb_ref[0, 0] = sb

    def finish(ci, carry):
        rows = _rows(ci, c)
        o = of_scr[rows, :] + ob_scr[rows, :]
        o = o * lax.rsqrt(jnp.mean(o * o, axis=-1, keepdims=True) + EPS)
        g = rg_ref[rows, :]
        y_ref[rows, :] = (g * _sigmoid(g)) * (o * gn_ref[...])
        return carry

    lax.fori_loop(0, nc, finish, 0)


def _retention(proj, seq_len, log_gamma, ret_gn, s_fwd, s_bwd, layer):
    n = proj.shape[0]
    nseq = n // seq_len
    hd = HEAD_DIM
    n_heads = log_gamma.shape[0]
    shared = s_fwd.shape[0] == 1
    col = lambda c0: pl.BlockSpec((seq_len, hd), lambda b, h: (b, c0 // hd + h))
    state = pl.BlockSpec((1, 1, 1, hd, hd), lambda b, h: (0 if shared else b, layer, 0 if shared else h, 0, 0))
    return pl.pallas_call(
        functools.partial(_retention_kernel, seq=seq_len),
        out_shape=[jax.ShapeDtypeStruct((n, n_heads * hd), F32),
                   jax.ShapeDtypeStruct((nseq, n_heads, hd, hd), F32),
                   jax.ShapeDtypeStruct((nseq, n_heads, hd, hd), F32)],
        grid=(nseq, n_heads),
        in_specs=[col(COL_RQ), col(COL_RK), col(COL_RV), col(COL_RG),
                  pl.BlockSpec((1, 2, hd), lambda b, h: (h, 0, 0)),
                  pl.BlockSpec((1, hd), lambda b, h: (0, h)),
                  state, state],
        out_specs=[pl.BlockSpec((seq_len, hd), lambda b, h: (b, h)),
                   pl.BlockSpec((1, 1, hd, hd), lambda b, h: (b, h, 0, 0)),
                   pl.BlockSpec((1, 1, hd, hd), lambda b, h: (b, h, 0, 0))],
        scratch_shapes=[pltpu.VMEM((seq_len, hd), F32), pltpu.VMEM((seq_len, hd), F32)],
        compiler_params=_cparams(2),
        name="retention",
    )(proj, proj, proj, proj, log_gamma, ret_gn.reshape(1, n_heads * hd), s_fwd, s_bwd)


def _merge_kernel(y0_ref, y1_ref, y2_ref, y3_ref, g0_ref, g1_ref, g2_ref, g3_ref, wb_ref, wo_ref,
                  x_ref, mod_ref, o_ref, yb_scr, acc):
    k = pl.program_id(1)

    @pl.when(k == 0)
    def _():
        for i, y_ref in enumerate((y0_ref, y1_ref, y2_ref, y3_ref)):
            yb_scr[i] = y_ref[...].astype(BF16)
        acc[...] = jnp.zeros(acc.shape, F32)

    m = g0_ref[...] * _dot(yb_scr[0], wb_ref[0])
    for i, g_ref in enumerate((g1_ref, g2_ref, g3_ref), start=1):
        m = m + g_ref[...] * _dot(yb_scr[i], wb_ref[i])
    acc[...] += _dot(m.astype(BF16), wo_ref[...])

    @pl.when(k == pl.num_programs(1) - 1)
    def _():
        o_ref[...] = x_ref[...] + mod_ref[0, 2:3, :] * acc[...]


def _merge(x, branches, gates, w_branch, w_out, mods, mod_row0, tiles_per_seq):
    n, d = x.shape
    bw = w_branch.shape[1]
    tm, tk = min(MERGE_TM, n), min(MERGE_TK, d)
    nk = d // tk
    tps = max(tiles_per_seq // tm, 1) if tiles_per_seq else n // tm
    gate = lambda i: pl.BlockSpec((tm, tk), lambda t, k: (t, i * nk + k))
    return pl.pallas_call(
        _merge_kernel,
        out_shape=jax.ShapeDtypeStruct((n, d), F32),
        grid=(n // tm, nk),
        in_specs=[pl.BlockSpec((tm, bw), lambda t, k: (t, 0))] * 4
                 + [gate(0), gate(1), gate(2), gate(3),
                    pl.BlockSpec((N_BRANCH, bw, tk), lambda t, k: (0, 0, k)),
                    pl.BlockSpec((tk, d), lambda t, k: (k, 0)),
                    pl.BlockSpec((tm, d), lambda t, k: (t, 0)),
                    pl.BlockSpec((1, 6, d), lambda t, k: (mod_row0 + t // tps, 0, 0))],
        out_specs=pl.BlockSpec((tm, d), lambda t, k: (t, 0)),
        scratch_shapes=[pltpu.VMEM((N_BRANCH, tm, bw), BF16), pltpu.VMEM((tm, d), F32)],
        compiler_params=_cparams(2),
        name="merge_out",
    )(*branches, gates, gates, gates, gates, w_branch, w_out, x, mods)


def _router_kernel(x_ref, mod_ref, g_ref, wt_ref, b_ref, h_ref, eid_ref, wts_ref):
    y = _rms(x_ref[...], g_ref[...])
    h = y * (1.0 + mod_ref[0, 4:5, :]) + mod_ref[0, 3:4, :]
    h_ref[...] = h
    h_hi = h.astype(BF16)
    h_lo = (h - h_hi.astype(F32)).astype(BF16)
    w = wt_ref[...]
    w_hi = w.astype(BF16)
    w_lo = (w - w_hi.astype(F32)).astype(BF16)
    logits = _dot_nt(w_hi, h_hi) + (_dot_nt(w_lo, h_hi) + _dot_nt(w_hi, h_lo))
    score = _sigmoid(logits)
    sel = score + b_ref[...]
    epg = N_EXPERTS // N_GROUPS
    s = [sel[e:e + 1, :] for e in range(N_EXPERTS)]
    sc = [score[e:e + 1, :] for e in range(N_EXPERTS)]

    def group_score(vals):
        best = None
        for a in range(len(vals)):
            for b in range(a + 1, len(vals)):
                pair = vals[a] + vals[b]
                best = pair if best is None else jnp.maximum(best, pair)
        return best

    gs = [group_score(s[g * epg:(g + 1) * epg]) for g in range(N_GROUPS)]
    g_best = jnp.zeros(gs[0].shape, jnp.int32)
    best = gs[0]
    for g in range(1, N_GROUPS):
        better = gs[g] > best
        g_best = jnp.where(better, g, g_best)
        best = jnp.where(better, gs[g], best)
    in_sel, in_score = [], []
    for k in range(epg):
        v, w_ = s[k], sc[k]
        for g in range(1, N_GROUPS):
            v = jnp.where(g_best == g, s[g * epg + k], v)
            w_ = jnp.where(g_best == g, sc[g * epg + k], w_)
        in_sel.append(v)
        in_score.append(w_)
    i1 = jnp.zeros(g_best.shape, jnp.int32)
    v1, w1 = in_sel[0], in_score[0]
    for k in range(1, epg):
        better = in_sel[k] > v1
        i1 = jnp.where(better, k, i1)
        v1 = jnp.where(better, in_sel[k], v1)
        w1 = jnp.where(better, in_score[k], w1)
    i2 = jnp.zeros(g_best.shape, jnp.int32)
    v2 = jnp.full(v1.shape, -jnp.inf, F32)
    w2 = jnp.zeros(v1.shape, F32)
    for k in range(epg):
        better = (i1 != k) & (in_sel[k] > v2)
        i2 = jnp.where(better, k, i2)
        v2 = jnp.where(better, in_sel[k], v2)
        w2 = jnp.where(better, in_score[k], w2)
    total = w1 + w2
    eid_ref[...] = jnp.concatenate([g_best * epg + i1, g_best * epg + i2], axis=0)
    wts_ref[...] = jnp.concatenate([w1 / total, w2 / total], axis=0)


def _router(x, mods, mod_row0, tiles_per_seq, gain, w_router_t, b_router):
    n, d = x.shape
    tm = min(ROUTER_TM, n)
    tps = max(tiles_per_seq // tm, 1) if tiles_per_seq else n // tm
    e = w_router_t.shape[0]
    return pl.pallas_call(
        _router_kernel,
        out_shape=[jax.ShapeDtypeStruct((n, d), F32),
                   jax.ShapeDtypeStruct((TOP_K, n), jnp.int32),
                   jax.ShapeDtypeStruct((TOP_K, n), F32)],
        grid=(n // tm,),
        in_specs=[pl.BlockSpec((tm, d), lambda i: (i, 0)),
                  pl.BlockSpec((1, 6, d), lambda i: (mod_row0 + i // tps, 0, 0)),
                  pl.BlockSpec((1, d), lambda i: (0, 0)),
                  pl.BlockSpec((e, d), lambda i: (0, 0)),
                  pl.BlockSpec((e, 1), lambda i: (0, 0))],
        out_specs=[pl.BlockSpec((tm, d), lambda i: (i, 0)),
                   pl.BlockSpec((TOP_K, tm), lambda i: (0, i)),
                   pl.BlockSpec((TOP_K, tm), lambda i: (0, i))],
        compiler_params=_cparams(1),
        name="router",
    )(x, mods, gain.reshape(1, d), w_router_t, b_router.reshape(e, 1))


def _row_copy(src_ref, src_row, dst_ref, dst_row, sem):
    return pltpu.make_async_copy(src_ref.at[pl.ds(src_row, 1)], dst_ref.at[pl.ds(dst_row, 1)], sem)


def _dispatch_kernel(slot_ref, h_ref, buf_in_ref, buf_ref, sem, *, tt):
    del buf_in_ref

    def copies(j):
        return [_row_copy(h_ref, j, buf_ref, slot_ref[0, 0, k * tt + j], sem) for k in range(TOP_K)]

    def start(j, c):
        for cp in copies(j):
            cp.start()
        return c

    def wait(j, c):
        for cp in copies(j):
            cp.wait()
        return c

    lax.fori_loop(0, tt, start, 0)
    lax.fori_loop(0, tt, wait, 0)


def _dispatch(h, slots, buf):
    n, d = h.shape
    tt = min(MOE_TT, n)
    return pl.pallas_call(
        functools.partial(_dispatch_kernel, tt=tt),
        out_shape=jax.ShapeDtypeStruct(buf.shape, buf.dtype),
        grid=(n // tt,),
        in_specs=[pl.BlockSpec((1, 1, TOP_K * tt), lambda i: (i, 0, 0), memory_space=pltpu.SMEM),
                  pl.BlockSpec((tt, d), lambda i: (i, 0)),
                  pl.BlockSpec(memory_space=pl.ANY)],
        out_specs=pl.BlockSpec(memory_space=pl.ANY),
        scratch_shapes=[pltpu.SemaphoreType.DMA],
        input_output_aliases={2: 0},
        compiler_params=_cparams(1),
        name="moe_dispatch",
    )(slots, h, buf)


def _ffn_kernel(be_ref, nu_ref, x_ref, wg_ref, wu_ref, wd_ref, o_ref):
    del be_ref
    live = pl.program_id(0) < nu_ref[0]

    @pl.when(live)
    def _():
        x = x_ref[...].astype(BF16)
        a = _dot(x, wg_ref[0])
        b = _dot(x, wu_ref[0])
        o_ref[...] = _dot(((a * _sigmoid(a)) * b).astype(BF16), wd_ref[0])

    @pl.when(jnp.logical_not(live))
    def _():
        o_ref[...] = jnp.zeros(o_ref.shape, F32)


def _expert_ffn(buf, block_expert, n_used, w_gate, w_up, w_down):
    rows, d = buf.shape
    de = w_gate.shape[2]
    rb = MOE_ROWS
    grid_spec = pltpu.PrefetchScalarGridSpec(
        num_scalar_prefetch=2,
        grid=(rows // rb,),
        in_specs=[pl.BlockSpec((rb, d), lambda i, be, nu: (i, 0)),
                  pl.BlockSpec((1, d, de), lambda i, be, nu: (be[i], 0, 0)),
                  pl.BlockSpec((1, d, de), lambda i, be, nu: (be[i], 0, 0)),
                  pl.BlockSpec((1, de, d), lambda i, be, nu: (be[i], 0, 0))],
        out_specs=pl.BlockSpec((rb, d), lambda i, be, nu: (i, 0)))
    return pl.pallas_call(
        _ffn_kernel,
        out_shape=jax.ShapeDtypeStruct((rows, d), F32),
        grid_spec=grid_spec,
        compiler_params=_cparams(1),
        name="expert_ffn",
    )(block_expert, n_used, buf, w_gate, w_up, w_down)


def _combine_kernel(slot_ref, yb_ref, x_ref, w_ref, mod_ref, o_ref, g0, g1, sem, *, tt):
    bufs = (g0, g1)

    def copies(j):
        return [_row_copy(yb_ref, slot_ref[0, 0, k * tt + j], bufs[k], j, sem) for k in range(TOP_K)]

    def start(j, c):
        for cp in copies(j):
            cp.start()
        return c

    def wait(j, c):
        for cp in copies(j):
            cp.wait()
        return c

    lax.fori_loop(0, tt, start, 0)
    lax.fori_loop(0, tt, wait, 0)
    w = w_ref[...]
    y = w[:, 0:1] * g0[...] + w[:, 1:2] * g1[...]
    o_ref[...] = x_ref[...] + mod_ref[0, 5:6, :] * y


def _combine(x, yb, slots, wts, mods, mod_row0, tiles_per_seq):
    n, d = x.shape
    tt = min(MOE_TT, n)
    tps = max(tiles_per_seq // tt, 1) if tiles_per_seq else n // tt
    return pl.pallas_call(
        functools.partial(_combine_kernel, tt=tt),
        out_shape=jax.ShapeDtypeStruct((n, d), F32),
        grid=(n // tt,),
        in_specs=[pl.BlockSpec((1, 1, TOP_K * tt), lambda i: (i, 0, 0), memory_space=pltpu.SMEM),
                  pl.BlockSpec(memory_space=pl.ANY),
                  pl.BlockSpec((tt, d), lambda i: (i, 0)),
                  pl.BlockSpec((tt, TOP_K), lambda i: (i, 0)),
                  pl.BlockSpec((1, 6, d), lambda i: (mod_row0 + i // tps, 0, 0))],
        out_specs=pl.BlockSpec((tt, d), lambda i: (i, 0)),
        scratch_shapes=[pltpu.VMEM((tt, d), F32), pltpu.VMEM((tt, d), F32), pltpu.SemaphoreType.DMA],
        compiler_params=_cparams(1),
        name="moe_combine",
    )(slots, yb, x, wts, mods)


def _slot_blocks(slots, tt):
    k, n = slots.shape
    return slots.reshape(k, n // tt, tt).transpose(1, 0, 2).reshape(n // tt, 1, k * tt)


def _moe(xs, mods, mod_rows, seq_lens, gain, w_router, b_router, w_gate, w_up, w_down):
    d = xs[0].shape[1]
    w_router_t = w_router.T
    routed = [_router(x, mods, r0, sl, gain, w_router_t, b_router) for x, r0, sl in zip(xs, mod_rows, seq_lens)]
    flat_e = jnp.concatenate([r[1].reshape(-1) for r in routed])
    n_assign = flat_e.shape[0]
    onehot = (flat_e[:, None] == jnp.arange(N_EXPERTS, dtype=jnp.int32)[None, :]).astype(jnp.int32)
    csum = jnp.cumsum(onehot, axis=0)
    counts = csum[-1]
    padded = (counts + MOE_ROWS - 1) // MOE_ROWS * MOE_ROWS
    pend = jnp.cumsum(padded)
    slot = jnp.sum(onehot * (csum - 1 + (pend - padded)[None, :]), axis=1)
    n_blocks = -(-n_assign // MOE_ROWS) + N_EXPERTS
    block_start = jnp.arange(n_blocks, dtype=jnp.int32) * MOE_ROWS
    block_expert = jnp.minimum(jnp.sum((pend[None, :] <= block_start[:, None]).astype(jnp.int32), axis=1),
                               N_EXPERTS - 1)
    n_used = (pend[-1:] // MOE_ROWS).astype(jnp.int32)
    buf = jnp.zeros((n_blocks * MOE_ROWS, d), F32)
    slot_blocks, off = [], 0
    for x, r in zip(xs, routed):
        n = x.shape[0]
        sb = _slot_blocks(slot[off:off + TOP_K * n].reshape(TOP_K, n), min(MOE_TT, n))
        off += TOP_K * n
        slot_blocks.append(sb)
        buf = _dispatch(r[0], sb, buf)
    yb = _expert_ffn(buf, block_expert, n_used, w_gate, w_up, w_down)
    return [_combine(x, yb, sb, r[2].T, mods, r0, sl)
            for x, r, sb, r0, sl in zip(xs, routed, slot_blocks, mod_rows, seq_lens)]


def _rope_tables(n_tok):
    t = jnp.arange(n_tok)
    row = (t // GRID_W).astype(F32)
    col = (t % GRID_W).astype(F32)
    quarter = HEAD_DIM // 4
    inv = ROPE_THETA ** (-jnp.arange(quarter, dtype=F32) / quarter)
    ar = row[:, None] * inv
    ac = col[:, None] * inv
    ang = jnp.concatenate([ar, ar, ac, ac], axis=-1)
    cos, sin = jnp.cos(ang), jnp.sin(ang)
    first = (jnp.arange(HEAD_DIM) % (2 * quarter)) < quarter
    return cos, jnp.where(first, -sin, 0.0), jnp.where(first, 0.0, sin)


def kernel(x_prompt, x_sample, cache_nat_k, cache_nat_v, cache_gqa_k, cache_gqa_v, state_ret_fwd,
           state_ret_bwd, c, c_ctx, w_mod, b_mod, norm1, norm2, w_in, conv_w, nat_qn, nat_kn, nat_rpb,
           gqa_qn, gqa_kn, ret_decay_fwd, ret_decay_bwd, ret_gn, w_branch, w_out, w_router, b_router,
           w_exp_gate, w_exp_up, w_exp_down):
    batch, seq, d = x_prompt.shape
    dec_batch, dec_seq, _ = x_sample.shape
    depth = w_mod.shape[0]
    n_heads = nat_rpb.shape[1]
    n_kv = cache_gqa_k.shape[2]
    hd = HEAD_DIM

    xp = x_prompt.reshape(batch * seq, d)
    xs = x_sample.reshape(dec_batch * dec_seq, d)
    cvecs = jnp.zeros((SUBLANES, d), F32).at[0].set(c_ctx).at[1:1 + dec_batch].set(c)
    mods_all = _modulation(cvecs, w_mod, b_mod).reshape(depth, SUBLANES, 6, d)
    rope_tabs = _rope_tables(dec_seq)
    zero_state = jnp.zeros((1, depth, 1, hd, hd), F32)

    caches = [[] for _ in range(6)]
    for l in range(depth):
        mods = mods_all[l]
        w_proj = w_in[l][:, :PROJ_W].astype(BF16)
        w_gatel = w_in[l][:, PROJ_W:].astype(BF16)
        wb = w_branch[l].astype(BF16)
        wo = w_out[l].astype(BF16)
        lg = jnp.stack([jax.nn.log_sigmoid(ret_decay_fwd[l].astype(F32)),
                        jax.nn.log_sigmoid(ret_decay_bwd[l].astype(F32))], axis=1)
        lg = jnp.broadcast_to(lg[:, :, None], (lg.shape[0], 2, hd))

        pc = _norm_project(xp, mods, 0, 0, norm1[l], w_proj, gate=False, name="in_proj_ctx")
        gc = _norm_project(xp, mods, 0, 0, norm1[l], w_gatel, gate=True, name="gate_proj_ctx")
        conv_c = _short_conv(pc, conv_w[l], seq)
        nat_c, nk, nv = _attention(pc, seq, n_heads, 1, COL_NQ, COL_NK, COL_NV, nat_qn[l], nat_kn[l],
                                   emit_kv=True, name="nat_ctx")
        gqa_c, gk, gv = _attention(pc, seq, n_kv, GQA_GROUP, COL_GQ, COL_GK, COL_GV, gqa_qn[l], gqa_kn[l],
                                   emit_kv=True, name="gqa_ctx")
        ret_c, s_f, s_b = _retention(pc, seq, lg, ret_gn[l], zero_state, zero_state, l)
        for lst, val in zip(caches, (nk, nv, gk, gv, s_f, s_b)):
            lst.append(val)
        xp1 = _merge(xp, (conv_c, nat_c, gqa_c, ret_c), gc, wb, wo, mods, 0, 0)

        pl_ = _norm_project(xs, mods, 1, dec_seq, norm1[l], w_proj, gate=False, name="in_proj_lat")
        gl = _norm_project(xs, mods, 1, dec_seq, norm1[l], w_gatel, gate=True, name="gate_proj_lat")
        conv_l = _short_conv(pl_, conv_w[l], dec_seq)
        nat_l = _natten(pl_, dec_seq, nat_qn[l], nat_kn[l], nat_rpb[l], cache_nat_k, cache_nat_v, l)
        gqa_l = _attention(pl_, dec_seq, n_kv, GQA_GROUP, COL_GQ, COL_GK, COL_GV, gqa_qn[l], gqa_kn[l],
                           rope_tabs=rope_tabs, cache=(cache_gqa_k, cache_gqa_v, l), name="gqa_lat")
        ret_l, _, _ = _retention(pl_, dec_seq, lg, ret_gn[l], state_ret_fwd, state_ret_bwd, l)
        xs1 = _merge(xs, (conv_l, nat_l, gqa_l, ret_l), gl, wb, wo, mods, 1, dec_seq)

        xp, xs = _moe([xp1, xs1], mods, [0, 1], [0, dec_seq], norm2[l], w_router, b_router,
                      w_exp_gate[l].astype(BF16), w_exp_up[l].astype(BF16), w_exp_down[l].astype(BF16))

    outs = [jnp.stack(v, axis=1) for v in caches]
    return (xp.reshape(batch, seq, d), xs.reshape(dec_batch, dec_seq, d), *outs)
```

```python
import functools

import numpy as np
import jax
import jax.numpy as jnp
from jax import lax
from jax.experimental import pallas as pl
from jax.experimental.pallas import tpu as pltpu

F32 = jnp.float32
BF16 = jnp.bfloat16

GRID_W = 64
HEAD_DIM = 128
BRANCH_W = 512
N_BRANCH = 4
CONV_K = 3
NAT_KR = 8
NAT_KC = 16
GQA_GROUP = 2
RET_CHUNK = 128
ROPE_THETA = 10000.0
N_EXPERTS = 16
N_GROUPS = 4
TOP_K = 2
EPS = 1e-6
MASKED = -1e30

COL_U, COL_BG, COL_CG = 0, 512, 1024
COL_NQ, COL_NK, COL_NV = 1536, 2048, 2560
COL_GQ, COL_GK, COL_GV = 3072, 3584, 3840
COL_RQ, COL_RK, COL_RV, COL_RG = 4096, 4608, 5120, 5632
PROJ_W = 6144

V7X_VMEM_LIMIT_BYTES = 56 * 1024 * 1024
SUBLANES = 8
LANES = 128

PROJ_TM, PROJ_TN = 1024, 1024
NORM_ROWS = 256
MERGE_TM, MERGE_TK = 1024, 512
ROUTER_TM = 512
MOE_ROWS = 256
MOE_TT = 256
CONV_TT = 1024
ATT_TQ, ATT_TK = 512, 1024
MOD_TN = 1024
NAT_ROWS = 8
RET_UNROLL = 4


def _cparams(n_axes):
    return pltpu.CompilerParams(dimension_semantics=("arbitrary",) * n_axes,
                                vmem_limit_bytes=V7X_VMEM_LIMIT_BYTES)


def _sigmoid(x):
    return 1.0 / (1.0 + jnp.exp(-x))


def _dot(a, b):
    return jnp.dot(a, b, preferred_element_type=F32)


def _dot_nt(a, b):
    return lax.dot_general(a, b, (((1,), (1,)), ((), ())), preferred_element_type=F32)


def _rms(x, gain):
    return x * lax.rsqrt(jnp.mean(x * x, axis=-1, keepdims=True) + EPS) * gain


def _rows(i, n):
    return pl.ds(pl.multiple_of(i * n, n), n)


def _mod_kernel(c_ref, w_ref, b_ref, o_ref):
    c = c_ref[...]
    a = (c * _sigmoid(c)).astype(BF16)
    o_ref[0] = _dot(a, w_ref[0].astype(BF16)) + b_ref[0]


def _modulation(cvecs, w_mod, b_mod):
    depth, d, n = w_mod.shape
    tn = min(MOD_TN, n)
    return pl.pallas_call(
        _mod_kernel,
        out_shape=jax.ShapeDtypeStruct((depth, cvecs.shape[0], n), F32),
        grid=(depth, n // tn),
        in_specs=[pl.BlockSpec(cvecs.shape, lambda l, j: (0, 0)),
                  pl.BlockSpec((1, d, tn), lambda l, j: (l, 0, j)),
                  pl.BlockSpec((1, 1, tn), lambda l, j: (l, 0, j))],
        out_specs=pl.BlockSpec((1, cvecs.shape[0], tn), lambda l, j: (l, 0, j)),
        compiler_params=_cparams(2),
        name="modulation",
    )(cvecs, w_mod, b_mod.reshape(depth, 1, n))


def _normproj_kernel(x_ref, mod_ref, g_ref, w_ref, o_ref, h_scr, *, gate):
    @pl.when(pl.program_id(1) == 0)
    def _():
        gain = g_ref[...] * (1.0 + mod_ref[0, 1:2, :])
        shift = mod_ref[0, 0:1, :]

        def chunk(i, c):
            rows = _rows(i, NORM_ROWS)
            x = x_ref[rows, :]
            y = x * lax.rsqrt(jnp.mean(x * x, axis=-1, keepdims=True) + EPS)
            h_scr[rows, :] = (y * gain + shift).astype(BF16)
            return c

        lax.fori_loop(0, x_ref.shape[0] // NORM_ROWS, chunk, 0)

    acc = _dot(h_scr[...], w_ref[...])
    o_ref[...] = _sigmoid(acc) if gate else acc


def _norm_project(x, mods, mod_row0, tiles_per_seq, gain, w, *, gate, name):
    n, d = x.shape
    nout = w.shape[1]
    tm, tn = min(PROJ_TM, n), min(PROJ_TN, nout)
    tps = max(tiles_per_seq // tm, 1) if tiles_per_seq else n // tm
    return pl.pallas_call(
        functools.partial(_normproj_kernel, gate=gate),
        out_shape=jax.ShapeDtypeStruct((n, nout), F32),
        grid=(n // tm, nout // tn),
        in_specs=[pl.BlockSpec((tm, d), lambda i, j: (i, 0)),
                  pl.BlockSpec((1, 6, d), lambda i, j: (mod_row0 + i // tps, 0, 0)),
                  pl.BlockSpec((1, d), lambda i, j: (0, 0)),
                  pl.BlockSpec((d, tn), lambda i, j: (0, j))],
        out_specs=pl.BlockSpec((tm, tn), lambda i, j: (i, j)),
        scratch_shapes=[pltpu.VMEM((tm, d), BF16)],
        compiler_params=_cparams(2),
        name=name,
    )(x, mods, gain.reshape(1, d), w)


def _conv_kernel(u_ref, bg_ref, cg_ref, up_ref, cp_ref, un_ref, cn_ref, w_ref, o_ref, *, tt, nt):
    t = pl.program_id(1)
    z = cg_ref[...] * u_ref[...]
    last = SUBLANES - 1
    z_before = jnp.where(t > 0, cp_ref[last:last + 1, :] * up_ref[last:last + 1, :], 0.0)
    z_after = jnp.where(t < nt - 1, cn_ref[0:1, :] * un_ref[0:1, :], 0.0)
    ri = lax.broadcasted_iota(jnp.int32, z.shape, 0)
    z_prev = jnp.where(ri == 0, z_before, pltpu.roll(z, 1, 0))
    z_next = jnp.where(ri == tt - 1, z_after, pltpu.roll(z, tt - 1, 0))
    w = w_ref[...]
    o_ref[...] = (bg_ref[...] * (w[0:1, :] * z_prev + w[1:2, :] * z + w[2:3, :] * z_next)).astype(BF16)


def _short_conv(proj, conv_w, seq_len):
    n = proj.shape[0]
    cw = conv_w.shape[1]
    nseq = n // seq_len
    tt = min(CONV_TT, seq_len)
    nt = seq_len // tt
    cu, cb, cc = COL_U // cw, COL_BG // cw, COL_CG // cw
    nblk8 = n // SUBLANES

    def main(col):
        return pl.BlockSpec((tt, cw), lambda b, t: (b * nt + t, col))

    def before(col):
        return pl.BlockSpec((SUBLANES, cw),
                            lambda b, t: (jnp.maximum((b * nt + t) * (tt // SUBLANES) - 1, 0), col))

    def after(col):
        return pl.BlockSpec((SUBLANES, cw),
                            lambda b, t: (jnp.minimum((b * nt + t + 1) * (tt // SUBLANES), nblk8 - 1), col))

    return pl.pallas_call(
        functools.partial(_conv_kernel, tt=tt, nt=nt),
        out_shape=jax.ShapeDtypeStruct((n, cw), BF16),
        grid=(nseq, nt),
        in_specs=[main(cu), main(cb), main(cc), before(cu), before(cc), after(cu), after(cc),
                  pl.BlockSpec(conv_w.shape, lambda b, t: (0, 0))],
        out_specs=pl.BlockSpec((tt, cw), lambda b, t: (b * nt + t, 0)),
        compiler_params=_cparams(2),
        name="short_conv",
    )(proj, proj, proj, proj, proj, proj, proj, conv_w)


def _attn_kernel(*refs, group, seq, tq, tk, rope, cache_len, emit_kv):
    it = iter(refs)
    q_ref, k_ref, v_ref, qn_ref, kn_ref = (next(it) for _ in range(5))
    if rope:
        cos_ref, sin_lo_ref, sin_hi_ref = (next(it) for _ in range(3))
    if cache_len:
        kc_ref, vc_ref = next(it), next(it)
    o_ref = next(it)
    if emit_kv:
        ko_ref, vo_ref = next(it), next(it)
    kb, vb, m_scr, l_scr, acc = (next(it) for _ in range(5))
    scale = HEAD_DIM ** -0.5
    hd = HEAD_DIM

    def prep(x, gain_ref, rows):
        y = _rms(x, gain_ref[...])
        if rope:
            y = (y * cos_ref[rows, :] + pltpu.roll(y, hd - hd // 4, 1) * sin_lo_ref[rows, :]
                 + pltpu.roll(y, hd // 4, 1) * sin_hi_ref[rows, :])
        return y

    def key_tile(kt, c):
        rows = _rows(kt, tk)
        kn = prep(k_ref[rows, :], kn_ref, rows)
        v = v_ref[rows, :]
        kb[rows, :] = kn.astype(BF16)
        vb[rows, :] = v.astype(BF16)
        if emit_kv:
            ko_ref[0, 0, rows, :] = kn
            vo_ref[0, 0, rows, :] = v
        return c

    lax.fori_loop(0, seq // tk, key_tile, 0)

    def softmax_step(q, kblk, vblk):
        s = _dot_nt(q, kblk)
        m_prev = m_scr[...]
        m_new = jnp.maximum(m_prev, jnp.max(s, axis=-1, keepdims=True))
        alpha = jnp.exp(m_prev - m_new)
        p = jnp.exp(s - jnp.concatenate([m_new] * (s.shape[1] // hd), axis=1))
        l_scr[...] = alpha * l_scr[...] + jnp.sum(p, axis=-1, keepdims=True)
        acc[...] = alpha * acc[...] + _dot(p.astype(BF16), vblk)
        m_scr[...] = m_new

    def query_tile(qt, c):
        rows = _rows(qt, tq)
        qs = [(prep(q_ref[rows, g * hd:(g + 1) * hd], qn_ref, rows) * scale).astype(BF16)
              for g in range(group)]
        q = qs[0] if group == 1 else jnp.concatenate(qs, axis=0)
        m_scr[...] = jnp.full(m_scr.shape, MASKED, F32)
        l_scr[...] = jnp.zeros(l_scr.shape, F32)
        acc[...] = jnp.zeros(acc.shape, F32)

        def kv_tile(kt, c2):
            r2 = _rows(kt, tk)
            softmax_step(q, kb[r2, :], vb[r2, :])
            return c2

        lax.fori_loop(0, seq // tk, kv_tile, 0)
        if cache_len:
            softmax_step(q, kc_ref[0, 0, 0].astype(BF16), vc_ref[0, 0, 0].astype(BF16))
        o = acc[...] / l_scr[...]
        for g in range(group):
            o_ref[rows, g * hd:(g + 1) * hd] = o[g * tq:(g + 1) * tq].astype(BF16)
        return c

    lax.fori_loop(0, seq // tq, query_tile, 0)


def _attention(proj, seq_len, n_kv, group, q_col, k_col, v_col, q_gain, k_gain, *,
               rope_tabs=None, cache=None, emit_kv=False, name):
    n = proj.shape[0]
    nseq = n // seq_len
    hd = HEAD_DIM
    tq, tk = min(ATT_TQ, seq_len), min(ATT_TK, seq_len)
    gw = group * hd
    in_specs = [pl.BlockSpec((seq_len, gw), lambda b, h: (b, q_col // gw + h)),
                pl.BlockSpec((seq_len, hd), lambda b, h: (b, k_col // hd + h)),
                pl.BlockSpec((seq_len, hd), lambda b, h: (b, v_col // hd + h)),
                pl.BlockSpec((1, hd), lambda b, h: (0, 0)),
                pl.BlockSpec((1, hd), lambda b, h: (0, 0))]
    args = [proj, proj, proj, q_gain.reshape(1, hd), k_gain.reshape(1, hd)]
    if rope_tabs is not None:
        in_specs += [pl.BlockSpec((seq_len, hd), lambda b, h: (0, 0))] * 3
        args += list(rope_tabs)
    cache_len = 0
    if cache is not None:
        kc, vc, layer = cache
        cache_len = kc.shape[3]
        in_specs += [pl.BlockSpec((1, 1, 1, cache_len, hd), lambda b, h: (b, layer, h, 0, 0))] * 2
        args += [kc, vc]
    out_shape = [jax.ShapeDtypeStruct((n, n_kv * gw), BF16)]
    out_specs = [pl.BlockSpec((seq_len, gw), lambda b, h: (b, h))]
    if emit_kv:
        out_shape += [jax.ShapeDtypeStruct((nseq, n_kv, seq_len, hd), F32)] * 2
        out_specs += [pl.BlockSpec((1, 1, seq_len, hd), lambda b, h: (b, h, 0, 0))] * 2
    m = group * tq
    res = pl.pallas_call(
        functools.partial(_attn_kernel, group=group, seq=seq_len, tq=tq, tk=tk,
                          rope=rope_tabs is not None, cache_len=cache_len, emit_kv=emit_kv),
        out_shape=out_shape,
        grid=(nseq, n_kv),
        in_specs=in_specs,
        out_specs=out_specs,
        scratch_shapes=[pltpu.VMEM((seq_len, hd), BF16), pltpu.VMEM((seq_len, hd), BF16),
                        pltpu.VMEM((m, hd), F32), pltpu.VMEM((m, hd), F32), pltpu.VMEM((m, hd), F32)],
        compiler_params=_cparams(2),
        name=name,
    )(*args)
    return res if emit_kv else res[0]


def _natten_kernel(q_ref, k_ref, v_ref, qn_ref, kn_ref, kc_ref, vc_ref, bias_ref, o_ref, kb, vb, *,
                   seq, width, kr, chunk):
    n_rows = seq // width
    scale = HEAD_DIM ** -0.5

    def key_chunk(i, c):
        rows = _rows(i, chunk)
        kb[rows, :] = _rms(k_ref[rows, :], kn_ref[...]).astype(BF16)
        vb[rows, :] = v_ref[rows, :].astype(BF16)
        return c

    lax.fori_loop(0, seq // chunk, key_chunk, 0)

    rg = NAT_ROWS if n_rows % NAT_ROWS == 0 else 1

    def row_group(gi, c):
        qrows = _rows(gi, rg * width)
        q = (_rms(q_ref[qrows, :], qn_ref[...]) * scale).astype(BF16)
        s_ctx = _dot_nt(q, kc_ref[0, 0, 0].astype(BF16))
        krows, s_loc = [], []
        for j in range(rg):
            r = gi * rg + j
            r0 = jnp.clip(r - kr // 2, 0, n_rows - kr)
            krows.append(pl.ds(pl.multiple_of(r0 * width, width), kr * width))
            s_loc.append(_dot_nt(q[j * width:(j + 1) * width], kb[krows[j], :]) + bias_ref[0, r - r0])
        s_loc = jnp.concatenate(s_loc, axis=0) if rg > 1 else s_loc[0]
        m = jnp.maximum(jnp.max(s_loc, axis=-1, keepdims=True), jnp.max(s_ctx, axis=-1, keepdims=True))
        p_loc = jnp.exp(s_loc - m)
        p_ctx = jnp.exp(s_ctx - m)
        denom = jnp.sum(p_loc, axis=-1, keepdims=True) + jnp.sum(p_ctx, axis=-1, keepdims=True)
        p_loc = p_loc.astype(BF16)
        o_loc = [_dot(p_loc[j * width:(j + 1) * width], vb[krows[j], :]) for j in range(rg)]
        o_loc = jnp.concatenate(o_loc, axis=0) if rg > 1 else o_loc[0]
        o = o_loc + _dot(p_ctx.astype(BF16), vc_ref[0, 0, 0].astype(BF16))
        o_ref[qrows, :] = (o / denom).astype(BF16)
        return c

    lax.fori_loop(0, n_rows // rg, row_group, 0)


def _natten_bias(rpb, width, kr):
    kc = NAT_KC
    cols = np.arange(width)
    c0 = np.clip(cols - kc // 2, 0, width - kc)
    j = np.arange(width)
    in_win = (j[None, :] >= c0[:, None]) & (j[None, :] < c0[:, None] + kc)
    ci = np.clip(j[None, :] - cols[:, None] + kc - 1, 0, 2 * kc - 2)
    off = np.arange(kr)
    ri = np.arange(kr)[None, :] - off[:, None] + NAT_KR - 1
    pick_r = np.zeros((kr * kr, 2 * NAT_KR - 1), np.float32)
    pick_r[np.arange(kr * kr), ri.reshape(-1)] = 1.0
    pick_c = np.zeros((2 * kc - 1, width * width), np.float32)
    pick_c[ci.reshape(-1), np.arange(width * width)] = 1.0
    tab = jnp.einsum('ar,hrs,sb->hab', pick_r, rpb.astype(F32), pick_c, precision=lax.Precision.HIGHEST)
    tab = tab.reshape(rpb.shape[0], kr, kr, width, width).transpose(0, 1, 3, 2, 4)
    tab = jnp.where(in_win[None, None, :, None, :], tab, MASKED)
    return tab.reshape(rpb.shape[0], kr, width, kr * width)


def _natten(proj, seq_len, q_gain, k_gain, rpb, kc, vc, layer):
    n = proj.shape[0]
    nseq = n // seq_len
    hd = HEAD_DIM
    n_heads = rpb.shape[0]
    width = GRID_W
    kr = min(NAT_KR, seq_len // width)
    bias = _natten_bias(rpb, width, kr)
    cache_len = kc.shape[3]
    chunk = min(512, seq_len)
    col = lambda c0: pl.BlockSpec((seq_len, hd), lambda b, h: (b, c0 // hd + h))
    return pl.pallas_call(
        functools.partial(_natten_kernel, seq=seq_len, width=width, kr=kr, chunk=chunk),
        out_shape=jax.ShapeDtypeStruct((n, n_heads * hd), BF16),
        grid=(nseq, n_heads),
        in_specs=[col(COL_NQ), col(COL_NK), col(COL_NV),
                  pl.BlockSpec((1, hd), lambda b, h: (0, 0)),
                  pl.BlockSpec((1, hd), lambda b, h: (0, 0)),
                  pl.BlockSpec((1, 1, 1, cache_len, hd), lambda b, h: (b, layer, h, 0, 0)),
                  pl.BlockSpec((1, 1, 1, cache_len, hd), lambda b, h: (b, layer, h, 0, 0)),
                  pl.BlockSpec((1, kr, width, kr * width), lambda b, h: (h, 0, 0, 0))],
        out_specs=pl.BlockSpec((seq_len, hd), lambda b, h: (b, h)),
        scratch_shapes=[pltpu.VMEM((seq_len, hd), BF16), pltpu.VMEM((seq_len, hd), BF16)],
        compiler_params=_cparams(2),
        name="natten_latent",
    )(proj, proj, proj, q_gain.reshape(1, hd), k_gain.reshape(1, hd), kc, vc, bias)


def _retention_kernel(q_ref, k_ref, v_ref, rg_ref, lg_ref, gn_ref, sf0_ref, sb0_ref,
                      y_ref, sf_ref, sb_ref, o_scr, kv_scr, sp_scr, *, seq, unroll):
    c = RET_CHUNK
    nc = seq // c
    hd = HEAD_DIM
    u = unroll
    lgf = lg_ref[0, 0:1, :]
    lgb = lg_ref[0, 1:2, :]
    ii = lax.broadcasted_iota(jnp.int32, (c, c), 0)
    jj = lax.broadcasted_iota(jnp.int32, (c, c), 1)
    rel = (ii - jj).astype(F32)
    pos = lax.broadcasted_iota(jnp.int32, (c, hd), 0).astype(F32)
    decay = (jnp.where(rel >= 0, jnp.exp(lgf * jnp.maximum(rel, 0.0)), 0.0)
             + jnp.where(rel <= 0, jnp.exp(lgb * jnp.maximum(-rel, 0.0)), 0.0))
    qd = jnp.concatenate([jnp.exp(lgf * (pos + 1.0)), jnp.exp(lgb * (c - pos))], axis=1)
    kd = jnp.concatenate([jnp.exp(lgf * (c - 1.0 - pos)), jnp.exp(lgb * pos)], axis=1)
    cd_f = jnp.exp(lgf * c)
    cd_b = jnp.exp(lgb * c)
    kscale = HEAD_DIM ** -0.5

    def within_chunks(gi, carry):
        rows = [_rows(gi * u + j, c) for j in range(u)]
        k = [k_ref[r, :] * kscale for r in rows]
        v = [v_ref[r, :].astype(BF16) for r in rows]
        qk = [_dot_nt(q_ref[rows[j], :].astype(BF16), k[j].astype(BF16)) for j in range(u)]
        att = [(s * decay).astype(BF16) for s in qk]
        for j in range(u):
            o_scr[rows[j], :] = _dot(att[j], v[j])
        for j in range(u):
            kk = jnp.concatenate([k[j], k[j]], axis=1) * kd
            kv_scr[gi * u + j] = _dot(kk.T.astype(BF16), v[j])
        return carry

    lax.fori_loop(0, nc // u, within_chunks, 0)

    def recurrence(t, carry):
        sf, sb = carry
        tb = nc - 1 - t
        sp_scr[t, 0:hd, :] = sf.astype(BF16)
        sp_scr[tb, hd:2 * hd, :] = sb.astype(BF16)
        return sf * cd_f + kv_scr[t, 0:hd, :], sb * cd_b + kv_scr[tb, hd:2 * hd, :]

    sf, sb = lax.fori_loop(0, nc, recurrence, (sf0_ref[0, 0, 0], sb0_ref[0, 0, 0]))
    sf_ref[0, 0] = sf
    sb_ref[0, 0] = sb

    def across_chunks(gi, carry):
        rows = [_rows(gi * u + j, c) for j in range(u)]
        o = []
        for j in range(u):
            q = q_ref[rows[j], :]
            qq = (jnp.concatenate([q, q], axis=1) * qd).astype(BF16)
            o.append(o_scr[rows[j], :] + _dot(qq, sp_scr[gi * u + j]))
        for j in range(u):
            on = o[j] * lax.rsqrt(jnp.mean(o[j] * o[j], axis=-1, keepdims=True) + EPS)
            g = rg_ref[rows[j], :]
            y_ref[rows[j], :] = ((g * _sigmoid(g)) * (on * gn_ref[...])).astype(BF16)
        return carry

    lax.fori_loop(0, nc // u, across_chunks, 0)


def _retention(proj, seq_len, log_gamma, ret_gn, s_fwd, s_bwd, layer):
    n = proj.shape[0]
    nseq = n // seq_len
    hd = HEAD_DIM
    n_heads = log_gamma.shape[0]
    nc = seq_len // RET_CHUNK
    shared = s_fwd.shape[0] == 1
    col = lambda c0: pl.BlockSpec((seq_len, hd), lambda b, h: (b, c0 // hd + h))
    state = pl.BlockSpec((1, 1, 1, hd, hd), lambda b, h: (0 if shared else b, layer, 0 if shared else h, 0, 0))
    return pl.pallas_call(
        functools.partial(_retention_kernel, seq=seq_len, unroll=next(u for u in (RET_UNROLL, 2, 1) if nc % u == 0)),
        out_shape=[jax.ShapeDtypeStruct((n, n_heads * hd), BF16),
                   jax.ShapeDtypeStruct((nseq, n_heads, hd, hd), F32),
                   jax.ShapeDtypeStruct((nseq, n_heads, hd, hd), F32)],
        grid=(nseq, n_heads),
        in_specs=[col(COL_RQ), col(COL_RK), col(COL_RV), col(COL_RG),
                  pl.BlockSpec((1, 2, hd), lambda b, h: (h, 0, 0)),
                  pl.BlockSpec((1, hd), lambda b, h: (0, h)),
                  state, state],
        out_specs=[pl.BlockSpec((seq_len, hd), lambda b, h: (b, h)),
                   pl.BlockSpec((1, 1, hd, hd), lambda b, h: (b, h, 0, 0)),
                   pl.BlockSpec((1, 1, hd, hd), lambda b, h: (b, h, 0, 0))],
        scratch_shapes=[pltpu.VMEM((seq_len, hd), F32), pltpu.VMEM((nc, 2 * hd, hd), F32),
                        pltpu.VMEM((nc, 2 * hd, hd), BF16)],
        compiler_params=_cparams(2),
        name="retention",
    )(proj, proj, proj, proj, log_gamma, ret_gn.reshape(1, n_heads * hd), s_fwd, s_bwd)


def _merge_kernel(y0_ref, y1_ref, y2_ref, y3_ref, g0_ref, g1_ref, g2_ref, g3_ref, wb_ref, wo_ref, o_ref):
    k = pl.program_id(1)
    m = g0_ref[...] * _dot(y0_ref[...], wb_ref[0])
    for i, (y_ref, g_ref) in enumerate(((y1_ref, g1_ref), (y2_ref, g2_ref), (y3_ref, g3_ref)), start=1):
        m = m + g_ref[...] * _dot(y_ref[...], wb_ref[i])
    part = _dot(m.astype(BF16), wo_ref[...])

    @pl.when(k == 0)
    def _():
        o_ref[...] = part

    @pl.when(k > 0)
    def _():
        o_ref[...] += part


def _merge(branches, gates, w_branch, w_out):
    n = gates.shape[0]
    _, bw, d = w_branch.shape
    tm, tk = min(MERGE_TM, n), min(MERGE_TK, d)
    nk = d // tk
    gate = lambda i: pl.BlockSpec((tm, tk), lambda t, k: (t, i * nk + k))
    return pl.pallas_call(
        _merge_kernel,
        out_shape=jax.ShapeDtypeStruct((n, d), F32),
        grid=(n // tm, nk),
        in_specs=[pl.BlockSpec((tm, bw), lambda t, k: (t, 0))] * 4
                 + [gate(0), gate(1), gate(2), gate(3),
                    pl.BlockSpec((N_BRANCH, bw, tk), lambda t, k: (0, 0, k)),
                    pl.BlockSpec((tk, d), lambda t, k: (k, 0))],
        out_specs=pl.BlockSpec((tm, d), lambda t, k: (t, 0)),
        compiler_params=_cparams(2),
        name="merge_out",
    )(*branches, gates, gates, gates, gates, w_branch, w_out)


def _router_kernel(x_ref, a_ref, mod_ref, g_ref, wt_ref, b_ref, x1_ref, h_ref, eid_ref, wts_ref):
    x1 = x_ref[...] + mod_ref[0, 2:3, :] * a_ref[...]
    x1_ref[...] = x1
    y = _rms(x1, g_ref[...])
    h = y * (1.0 + mod_ref[0, 4:5, :]) + mod_ref[0, 3:4, :]
    h_ref[...] = h
    h_hi = h.astype(BF16)
    h_lo = (h - h_hi.astype(F32)).astype(BF16)
    w = wt_ref[...]
    w_hi = w.astype(BF16)
    w_lo = (w - w_hi.astype(F32)).astype(BF16)
    logits = _dot_nt(w_hi, h_hi) + (_dot_nt(w_lo, h_hi) + _dot_nt(w_hi, h_lo))
    score = _sigmoid(logits)
    sel = score + b_ref[...]
    epg = N_EXPERTS // N_GROUPS
    s = [sel[e:e + 1, :] for e in range(N_EXPERTS)]
    sc = [score[e:e + 1, :] for e in range(N_EXPERTS)]

    def group_score(vals):
        best = None
        for a in range(len(vals)):
            for b in range(a + 1, len(vals)):
                pair = vals[a] + vals[b]
                best = pair if best is None else jnp.maximum(best, pair)
        return best

    gs = [group_score(s[g * epg:(g + 1) * epg]) for g in range(N_GROUPS)]
    g_best = jnp.zeros(gs[0].shape, jnp.int32)
    best = gs[0]
    for g in range(1, N_GROUPS):
        better = gs[g] > best
        g_best = jnp.where(better, g, g_best)
        best = jnp.where(better, gs[g], best)
    in_sel, in_score = [], []
    for k in range(epg):
        v, w_ = s[k], sc[k]
        for g in range(1, N_GROUPS):
            v = jnp.where(g_best == g, s[g * epg + k], v)
            w_ = jnp.where(g_best == g, sc[g * epg + k], w_)
        in_sel.append(v)
        in_score.append(w_)
    i1 = jnp.zeros(g_best.shape, jnp.int32)
    v1, w1 = in_sel[0], in_score[0]
    for k in range(1, epg):
        better = in_sel[k] > v1
        i1 = jnp.where(better, k, i1)
        v1 = jnp.where(better, in_sel[k], v1)
        w1 = jnp.where(better, in_score[k], w1)
    i2 = jnp.zeros(g_best.shape, jnp.int32)
    v2 = jnp.full(v1.shape, -jnp.inf, F32)
    w2 = jnp.zeros(v1.shape, F32)
    for k in range(epg):
        better = (i1 != k) & (in_sel[k] > v2)
        i2 = jnp.where(better, k, i2)
        v2 = jnp.where(better, in_sel[k], v2)
        w2 = jnp.where(better, in_score[k], w2)
    total = w1 + w2
    eid_ref[...] = jnp.concatenate([g_best * epg + i1, g_best * epg + i2], axis=0)
    wts_ref[...] = jnp.concatenate([w1 / total, w2 / total], axis=0)


def _router(x, mixed, mods, mod_row0, tiles_per_seq, gain, w_router_t, b_router):
    n, d = x.shape
    tm = min(ROUTER_TM, n)
    tps = max(tiles_per_seq // tm, 1) if tiles_per_seq else n // tm
    e = w_router_t.shape[0]
    tile = pl.BlockSpec((tm, d), lambda i: (i, 0))
    return pl.pallas_call(
        _router_kernel,
        out_shape=[jax.ShapeDtypeStruct((n, d), F32),
                   jax.ShapeDtypeStruct((n, d), F32),
                   jax.ShapeDtypeStruct((TOP_K, n), jnp.int32),
                   jax.ShapeDtypeStruct((TOP_K, n), F32)],
        grid=(n // tm,),
        in_specs=[tile, tile,
                  pl.BlockSpec((1, 6, d), lambda i: (mod_row0 + i // tps, 0, 0)),
                  pl.BlockSpec((1, d), lambda i: (0, 0)),
                  pl.BlockSpec((e, d), lambda i: (0, 0)),
                  pl.BlockSpec((e, 1), lambda i: (0, 0))],
        out_specs=[tile, tile,
                   pl.BlockSpec((TOP_K, tm), lambda i: (0, i)),
                   pl.BlockSpec((TOP_K, tm), lambda i: (0, i))],
        compiler_params=_cparams(1),
        name="router",
    )(x, mixed, mods, gain.reshape(1, d), w_router_t, b_router.reshape(e, 1))


def _row_copy(src_ref, src_row, dst_ref, dst_row, sem):
    return pltpu.make_async_copy(src_ref.at[pl.ds(src_row, 1)], dst_ref.at[pl.ds(dst_row, 1)], sem)


def _dispatch_kernel(slot_ref, h_ref, buf_in_ref, buf_ref, sem, *, tt):
    del buf_in_ref

    def copies(j):
        return [_row_copy(h_ref, j, buf_ref, slot_ref[0, 0, k * tt + j], sem) for k in range(TOP_K)]

    def start(j, c):
        for k, cp in enumerate(copies(j)):
            cp.start(priority=k % 2)
        return c

    def wait(j, c):
        for cp in copies(j):
            cp.wait()
        return c

    lax.fori_loop(0, tt, start, 0)
    lax.fori_loop(0, tt, wait, 0)


def _dispatch(h, slots, buf):
    n, d = h.shape
    tt = min(MOE_TT, n)
    return pl.pallas_call(
        functools.partial(_dispatch_kernel, tt=tt),
        out_shape=jax.ShapeDtypeStruct(buf.shape, buf.dtype),
        grid=(n // tt,),
        in_specs=[pl.BlockSpec((1, 1, TOP_K * tt), lambda i: (i, 0, 0), memory_space=pltpu.SMEM),
                  pl.BlockSpec((tt, d), lambda i: (i, 0)),
                  pl.BlockSpec(memory_space=pl.ANY)],
        out_specs=pl.BlockSpec(memory_space=pl.ANY),
        scratch_shapes=[pltpu.SemaphoreType.DMA],
        input_output_aliases={2: 0},
        compiler_params=_cparams(1),
        name="moe_dispatch",
    )(slots, h, buf)


def _ffn_kernel(be_ref, nu_ref, x_ref, wg_ref, wu_ref, wd_ref, o_ref):
    del be_ref
    live = pl.program_id(0) < nu_ref[0]

    @pl.when(live)
    def _():
        x = x_ref[...].astype(BF16)
        a = _dot(x, wg_ref[0])
        b = _dot(x, wu_ref[0])
        o_ref[...] = _dot(((a * _sigmoid(a)) * b).astype(BF16), wd_ref[0])

    @pl.when(jnp.logical_not(live))
    def _():
        o_ref[...] = jnp.zeros(o_ref.shape, F32)


def _expert_ffn(buf, block_expert, n_used, w_gate, w_up, w_down):
    rows, d = buf.shape
    de = w_gate.shape[2]
    rb = MOE_ROWS
    grid_spec = pltpu.PrefetchScalarGridSpec(
        num_scalar_prefetch=2,
        grid=(rows // rb,),
        in_specs=[pl.BlockSpec((rb, d), lambda i, be, nu: (i, 0)),
                  pl.BlockSpec((1, d, de), lambda i, be, nu: (be[i], 0, 0)),
                  pl.BlockSpec((1, d, de), lambda i, be, nu: (be[i], 0, 0)),
                  pl.BlockSpec((1, de, d), lambda i, be, nu: (be[i], 0, 0))],
        out_specs=pl.BlockSpec((rb, d), lambda i, be, nu: (i, 0)))
    return pl.pallas_call(
        _ffn_kernel,
        out_shape=jax.ShapeDtypeStruct((rows, d), F32),
        grid_spec=grid_spec,
        compiler_params=_cparams(1),
        name="expert_ffn",
    )(block_expert, n_used, buf, w_gate, w_up, w_down)


def _combine_kernel(slot_ref, yb_ref, x_ref, w_ref, mod_ref, o_ref, g0, g1, sem, *, tt):
    bufs = (g0, g1)

    def copies(j):
        return [_row_copy(yb_ref, slot_ref[0, 0, k * tt + j], bufs[k], j, sem) for k in range(TOP_K)]

    def start(j, c):
        for k, cp in enumerate(copies(j)):
            cp.start(priority=k % 2)
        return c

    def wait(j, c):
        for cp in copies(j):
            cp.wait()
        return c

    lax.fori_loop(0, tt, start, 0)
    lax.fori_loop(0, tt, wait, 0)
    w = w_ref[...]
    y = w[:, 0:1] * g0[...] + w[:, 1:2] * g1[...]
    o_ref[...] = x_ref[...] + mod_ref[0, 5:6, :] * y


def _combine(x, yb, slots, wts, mods, mod_row0, tiles_per_seq):
    n, d = x.shape
    tt = min(MOE_TT, n)
    tps = max(tiles_per_seq // tt, 1) if tiles_per_seq else n // tt
    return pl.pallas_call(
        functools.partial(_combine_kernel, tt=tt),
        out_shape=jax.ShapeDtypeStruct((n, d), F32),
        grid=(n // tt,),
        in_specs=[pl.BlockSpec((1, 1, TOP_K * tt), lambda i: (i, 0, 0), memory_space=pltpu.SMEM),
                  pl.BlockSpec(memory_space=pl.ANY),
                  pl.BlockSpec((tt, d), lambda i: (i, 0)),
                  pl.BlockSpec((tt, TOP_K), lambda i: (i, 0)),
                  pl.BlockSpec((1, 6, d), lambda i: (mod_row0 + i // tps, 0, 0))],
        out_specs=pl.BlockSpec((tt, d), lambda i: (i, 0)),
        scratch_shapes=[pltpu.VMEM((tt, d), F32), pltpu.VMEM((tt, d), F32), pltpu.SemaphoreType.DMA],
        compiler_params=_cparams(1),
        name="moe_combine",
    )(slots, yb, x, wts, mods)


def _slot_blocks(slots, tt):
    k, n = slots.shape
    return slots.reshape(k, n // tt, tt).transpose(1, 0, 2).reshape(n // tt, 1, k * tt)


def _moe(xs, mixed, mods, mod_rows, seq_lens, gain, w_router, b_router, w_gate, w_up, w_down):
    d = xs[0].shape[1]
    w_router_t = w_router.T
    routed = [_router(x, a, mods, r0, sl, gain, w_router_t, b_router)
              for x, a, r0, sl in zip(xs, mixed, mod_rows, seq_lens)]
    flat_e = jnp.concatenate([r[2].reshape(-1) for r in routed])
    n_assign = flat_e.shape[0]
    onehot = (flat_e[:, None] == jnp.arange(N_EXPERTS, dtype=jnp.int32)[None, :]).astype(jnp.int32)
    csum = jnp.cumsum(onehot, axis=0)
    counts = csum[-1]
    padded = (counts + MOE_ROWS - 1) // MOE_ROWS * MOE_ROWS
    pend = jnp.cumsum(padded)
    slot = jnp.sum(onehot * (csum - 1 + (pend - padded)[None, :]), axis=1)
    n_blocks = -(-n_assign // MOE_ROWS) + N_EXPERTS
    block_start = jnp.arange(n_blocks, dtype=jnp.int32) * MOE_ROWS
    block_expert = jnp.minimum(jnp.sum((pend[None, :] <= block_start[:, None]).astype(jnp.int32), axis=1),
                               N_EXPERTS - 1)
    n_used = (pend[-1:] // MOE_ROWS).astype(jnp.int32)
    buf = jnp.zeros((n_blocks * MOE_ROWS, d), F32)
    slot_blocks, off = [], 0
    for r in routed:
        n = r[0].shape[0]
        sb = _slot_blocks(slot[off:off + TOP_K * n].reshape(TOP_K, n), min(MOE_TT, n))
        off += TOP_K * n
        slot_blocks.append(sb)
        buf = _dispatch(r[1], sb, buf)
    yb = _expert_ffn(buf, block_expert, n_used, w_gate, w_up, w_down)
    return [_combine(r[0], yb, sb, r[3].T, mods, r0, sl)
            for r, sb, r0, sl in zip(routed, slot_blocks, mod_rows, seq_lens)]


def _rope_tables(n_tok):
    t = jnp.arange(n_tok)
    row = (t // GRID_W).astype(F32)
    col = (t % GRID_W).astype(F32)
    quarter = HEAD_DIM // 4
    inv = ROPE_THETA ** (-jnp.arange(quarter, dtype=F32) / quarter)
    ar = row[:, None] * inv
    ac = col[:, None] * inv
    ang = jnp.concatenate([ar, ar, ac, ac], axis=-1)
    cos, sin = jnp.cos(ang), jnp.sin(ang)
    first = (jnp.arange(HEAD_DIM) % (2 * quarter)) < quarter
    return cos, jnp.where(first, -sin, 0.0), jnp.where(first, 0.0, sin)


def kernel(x_prompt, x_sample, cache_nat_k, cache_nat_v, cache_gqa_k, cache_gqa_v, state_ret_fwd,
           state_ret_bwd, c, c_ctx, w_mod, b_mod, norm1, norm2, w_in, conv_w, nat_qn, nat_kn, nat_rpb,
           gqa_qn, gqa_kn, ret_decay_fwd, ret_decay_bwd, ret_gn, w_branch, w_out, w_router, b_router,
           w_exp_gate, w_exp_up, w_exp_down):
    batch, seq, d = x_prompt.shape
    dec_batch, dec_seq, _ = x_sample.shape
    depth = w_mod.shape[0]
    n_heads = nat_rpb.shape[1]
    n_kv = cache_gqa_k.shape[2]
    hd = HEAD_DIM

    xp = x_prompt.reshape(batch * seq, d)
    xs = x_sample.reshape(dec_batch * dec_seq, d)
    cvecs = jnp.zeros((SUBLANES, d), F32).at[0].set(c_ctx).at[1:1 + dec_batch].set(c)
    mods_all = _modulation(cvecs, w_mod, b_mod).reshape(depth, SUBLANES, 6, d)
    rope_tabs = _rope_tables(dec_seq)
    zero_state = jnp.zeros((1, depth, 1, hd, hd), F32)

    caches = [[] for _ in range(6)]
    for l in range(depth):
        mods = mods_all[l]
        w_proj = w_in[l][:, :PROJ_W].astype(BF16)
        w_gatel = w_in[l][:, PROJ_W:].astype(BF16)
        wb = w_branch[l].astype(BF16)
        wo = w_out[l].astype(BF16)
        lg = jnp.stack([jax.nn.log_sigmoid(ret_decay_fwd[l].astype(F32)),
                        jax.nn.log_sigmoid(ret_decay_bwd[l].astype(F32))], axis=1)
        lg = jnp.broadcast_to(lg[:, :, None], (lg.shape[0], 2, hd))

        pc = _norm_project(xp, mods, 0, 0, norm1[l], w_proj, gate=False, name="in_proj_ctx")
        gc = _norm_project(xp, mods, 0, 0, norm1[l], w_gatel, gate=True, name="gate_proj_ctx")
        conv_c = _short_conv(pc, conv_w[l], seq)
        nat_c, nk, nv = _attention(pc, seq, n_heads, 1, COL_NQ, COL_NK, COL_NV, nat_qn[l], nat_kn[l],
                                   emit_kv=True, name="nat_ctx")
        gqa_c, gk, gv = _attention(pc, seq, n_kv, GQA_GROUP, COL_GQ, COL_GK, COL_GV, gqa_qn[l], gqa_kn[l],
                                   emit_kv=True, name="gqa_ctx")
        ret_c, s_f, s_b = _retention(pc, seq, lg, ret_gn[l], zero_state, zero_state, l)
        for lst, val in zip(caches, (nk, nv, gk, gv, s_f, s_b)):
            lst.append(val)
        mix_c = _merge((conv_c, nat_c, gqa_c, ret_c), gc, wb, wo)

        pl_ = _norm_project(xs, mods, 1, dec_seq, norm1[l], w_proj, gate=False, name="in_proj_lat")
        gl = _norm_project(xs, mods, 1, dec_seq, norm1[l], w_gatel, gate=True, name="gate_proj_lat")
        conv_l = _short_conv(pl_, conv_w[l], dec_seq)
        nat_l = _natten(pl_, dec_seq, nat_qn[l], nat_kn[l], nat_rpb[l], cache_nat_k, cache_nat_v, l)
        gqa_l = _attention(pl_, dec_seq, n_kv, GQA_GROUP, COL_GQ, COL_GK, COL_GV, gqa_qn[l], gqa_kn[l],
                           rope_tabs=rope_tabs, cache=(cache_gqa_k, cache_gqa_v, l), name="gqa_lat")
        ret_l, _, _ = _retention(pl_, dec_seq, lg, ret_gn[l], state_ret_fwd, state_ret_bwd, l)
        mix_l = _merge((conv_l, nat_l, gqa_l, ret_l), gl, wb, wo)

        xp, xs = _moe([xp, xs], [mix_c, mix_l], mods, [0, 1], [0, dec_seq], norm2[l], w_router, b_router,
                      w_exp_gate[l].astype(BF16), w_exp_up[l].astype(BF16), w_exp_down[l].astype(BF16))

    outs = [jnp.stack(v, axis=1) for v in caches]
    return (xp.reshape(batch, seq, d), xs.reshape(dec_batch, dec_seq, d), *outs)
```

```python
import functools

import numpy as np
import jax
import jax.numpy as jnp
from jax import lax
from jax.experimental import pallas as pl
from jax.experimental.pallas import tpu as pltpu

F32 = jnp.float32
BF16 = jnp.bfloat16

GRID_W = 64
HEAD_DIM = 128
BRANCH_W = 512
N_BRANCH = 4
CONV_K = 3
NAT_KR = 8
NAT_KC = 16
GQA_GROUP = 2
RET_CHUNK = 128
ROPE_THETA = 10000.0
N_EXPERTS = 16
N_GROUPS = 4
TOP_K = 2
EPS = 1e-6
MASKED = -1e30
LOG2_E = 1.4426950408889634

COL_U, COL_BG, COL_CG = 0, 512, 1024
COL_NQ, COL_NK, COL_NV = 1536, 2048, 2560
COL_GQ, COL_GK, COL_GV = 3072, 3584, 3840
COL_RQ, COL_RK, COL_RV, COL_RG = 4096, 4608, 5120, 5632
PROJ_W = 6144

V7X_VMEM_LIMIT_BYTES = 56 * 1024 * 1024
SUBLANES = 8
LANES = 128

PROJ_TM, PROJ_TN = 1024, 1024
NORM_ROWS = 256
MERGE_TM, MERGE_TK = 1024, 512
ROUTER_TM = 512
MOE_ROWS = 256
MOE_TT = 256
CONV_TT = 1024
ATT_TQ, ATT_TK = 512, 1024
MOD_TN = 1024
NAT_ROWS = 8
RET_UNROLL = 4
DMA_UNROLL = 8


def _cparams(n_axes):
    return pltpu.CompilerParams(dimension_semantics=("arbitrary",) * n_axes,
                                vmem_limit_bytes=V7X_VMEM_LIMIT_BYTES)


def _sigmoid(x):
    return 1.0 / (1.0 + jnp.exp(-x))


def _dot(a, b):
    return jnp.dot(a, b, preferred_element_type=F32)


def _dot_nt(a, b):
    return lax.dot_general(a, b, (((1,), (1,)), ((), ())), preferred_element_type=F32)


def _rms(x, gain):
    return x * lax.rsqrt(jnp.mean(x * x, axis=-1, keepdims=True) + EPS) * gain


def _rows(i, n):
    return pl.ds(pl.multiple_of(i * n, n), n)


def _mod_kernel(c_ref, w_ref, b_ref, o_ref):
    c = c_ref[...]
    a = (c * _sigmoid(c)).astype(BF16)
    o_ref[0] = _dot(a, w_ref[0].astype(BF16)) + b_ref[0]


def _modulation(cvecs, w_mod, b_mod):
    depth, d, n = w_mod.shape
    tn = min(MOD_TN, n)
    return pl.pallas_call(
        _mod_kernel,
        out_shape=jax.ShapeDtypeStruct((depth, cvecs.shape[0], n), F32),
        grid=(depth, n // tn),
        in_specs=[pl.BlockSpec(cvecs.shape, lambda l, j: (0, 0)),
                  pl.BlockSpec((1, d, tn), lambda l, j: (l, 0, j)),
                  pl.BlockSpec((1, 1, tn), lambda l, j: (l, 0, j))],
        out_specs=pl.BlockSpec((1, cvecs.shape[0], tn), lambda l, j: (l, 0, j)),
        compiler_params=_cparams(2),
        name="modulation",
    )(cvecs, w_mod, b_mod.reshape(depth, 1, n))


def _normproj_kernel(x_ref, mod_ref, g_ref, w_ref, o_ref, h_scr, *, gate):
    @pl.when(pl.program_id(1) == 0)
    def _():
        gain = g_ref[...] * (1.0 + mod_ref[0, 1:2, :])
        shift = mod_ref[0, 0:1, :]

        def chunk(i, c):
            rows = _rows(i, NORM_ROWS)
            x = x_ref[rows, :]
            y = x * lax.rsqrt(jnp.mean(x * x, axis=-1, keepdims=True) + EPS)
            h_scr[rows, :] = (y * gain + shift).astype(BF16)
            return c

        lax.fori_loop(0, x_ref.shape[0] // NORM_ROWS, chunk, 0)

    acc = _dot(h_scr[...], w_ref[...])
    o_ref[...] = _sigmoid(acc) if gate else acc


def _norm_project(x, mods, mod_row0, tiles_per_seq, gain, w, *, gate, name):
    n, d = x.shape
    nout = w.shape[1]
    tm, tn = min(PROJ_TM, n), min(PROJ_TN, nout)
    tps = max(tiles_per_seq // tm, 1) if tiles_per_seq else n // tm
    return pl.pallas_call(
        functools.partial(_normproj_kernel, gate=gate),
        out_shape=jax.ShapeDtypeStruct((n, nout), F32),
        grid=(n // tm, nout // tn),
        in_specs=[pl.BlockSpec((tm, d), lambda i, j: (i, 0)),
                  pl.BlockSpec((1, 6, d), lambda i, j: (mod_row0 + i // tps, 0, 0)),
                  pl.BlockSpec((1, d), lambda i, j: (0, 0)),
                  pl.BlockSpec((d, tn), lambda i, j: (0, j))],
        out_specs=pl.BlockSpec((tm, tn), lambda i, j: (i, j)),
        scratch_shapes=[pltpu.VMEM((tm, d), BF16)],
        compiler_params=_cparams(2),
        name=name,
    )(x, mods, gain.reshape(1, d), w)


def _conv_kernel(u_ref, bg_ref, cg_ref, up_ref, cp_ref, un_ref, cn_ref, w_ref, o_ref, *, tt, nt):
    t = pl.program_id(1)
    z = cg_ref[...] * u_ref[...]
    last = SUBLANES - 1
    z_before = jnp.where(t > 0, cp_ref[last:last + 1, :] * up_ref[last:last + 1, :], 0.0)
    z_after = jnp.where(t < nt - 1, cn_ref[0:1, :] * un_ref[0:1, :], 0.0)
    ri = lax.broadcasted_iota(jnp.int32, z.shape, 0)
    z_prev = jnp.where(ri == 0, z_before, pltpu.roll(z, 1, 0))
    z_next = jnp.where(ri == tt - 1, z_after, pltpu.roll(z, tt - 1, 0))
    w = w_ref[...]
    o_ref[...] = (bg_ref[...] * (w[0:1, :] * z_prev + w[1:2, :] * z + w[2:3, :] * z_next)).astype(BF16)


def _short_conv(proj, conv_w, seq_len):
    n = proj.shape[0]
    cw = conv_w.shape[1]
    nseq = n // seq_len
    tt = min(CONV_TT, seq_len)
    nt = seq_len // tt
    cu, cb, cc = COL_U // cw, COL_BG // cw, COL_CG // cw
    nblk8 = n // SUBLANES

    def main(col):
        return pl.BlockSpec((tt, cw), lambda b, t: (b * nt + t, col))

    def before(col):
        return pl.BlockSpec((SUBLANES, cw),
                            lambda b, t: (jnp.maximum((b * nt + t) * (tt // SUBLANES) - 1, 0), col))

    def after(col):
        return pl.BlockSpec((SUBLANES, cw),
                            lambda b, t: (jnp.minimum((b * nt + t + 1) * (tt // SUBLANES), nblk8 - 1), col))

    return pl.pallas_call(
        functools.partial(_conv_kernel, tt=tt, nt=nt),
        out_shape=jax.ShapeDtypeStruct((n, cw), BF16),
        grid=(nseq, nt),
        in_specs=[main(cu), main(cb), main(cc), before(cu), before(cc), after(cu), after(cc),
                  pl.BlockSpec(conv_w.shape, lambda b, t: (0, 0))],
        out_specs=pl.BlockSpec((tt, cw), lambda b, t: (b * nt + t, 0)),
        compiler_params=_cparams(2),
        name="short_conv",
    )(proj, proj, proj, proj, proj, proj, proj, conv_w)


def _attn_kernel(*refs, group, seq, tq, tk, rope, cache_len, emit_kv):
    it = iter(refs)
    q_ref, k_ref, v_ref, qn_ref, kn_ref = (next(it) for _ in range(5))
    if rope:
        cos_ref, sin_lo_ref, sin_hi_ref = (next(it) for _ in range(3))
    if cache_len:
        kc_ref, vc_ref = next(it), next(it)
    o_ref = next(it)
    if emit_kv:
        ko_ref, vo_ref = next(it), next(it)
    kb, vb, m_scr, l_scr, acc, q_scr = (next(it) for _ in range(6))
    scale = HEAD_DIM ** -0.5 * LOG2_E
    hd = HEAD_DIM

    def prep(x, gain_ref, rows):
        y = _rms(x, gain_ref[...])
        if rope:
            y = (y * cos_ref[rows, :] + pltpu.roll(y, hd - hd // 4, 1) * sin_lo_ref[rows, :]
                 + pltpu.roll(y, hd // 4, 1) * sin_hi_ref[rows, :])
        return y

    def key_tile(kt, c):
        rows = _rows(kt, tk)
        kn = prep(k_ref[rows, :], kn_ref, rows)
        v = v_ref[rows, :]
        kb[rows, :] = kn.astype(BF16)
        vb[rows, :] = v.astype(BF16)
        if emit_kv:
            ko_ref[0, 0, rows, :] = kn
            vo_ref[0, 0, rows, :] = v
        return c

    lax.fori_loop(0, seq // tk, key_tile, 0)

    def softmax_step(s, vblk):
        m_prev = m_scr[...]
        m_new = jnp.maximum(m_prev, jnp.max(s, axis=-1, keepdims=True))
        alpha = jnp.exp2(m_prev - m_new)
        p = jnp.exp2(s - jnp.concatenate([m_new] * (s.shape[1] // hd), axis=1))
        l_scr[...] = alpha * l_scr[...] + jnp.sum(p, axis=-1, keepdims=True)
        acc[...] = alpha * acc[...] + _dot(p.astype(BF16), vblk)
        m_scr[...] = m_new

    def query_tile(qt, c):
        rows = _rows(qt, tq)
        for g in range(group):
            q_scr[g * tq:(g + 1) * tq, :] = (
                prep(q_ref[rows, g * hd:(g + 1) * hd], qn_ref, rows) * scale).astype(BF16)
        m_scr[...] = jnp.full(m_scr.shape, MASKED, F32)
        l_scr[...] = jnp.zeros(l_scr.shape, F32)
        acc[...] = jnp.zeros(acc.shape, F32)
        n_kv = seq // tk

        def scores(kt):
            return _dot_nt(q_scr[...], kb[_rows(kt, tk), :])

        def kv_tile(kt, c2):
            softmax_step(scores(kt), vb[_rows(kt, tk), :])
            return c2

        lax.fori_loop(0, n_kv, kv_tile, 0)
        if cache_len:
            softmax_step(_dot_nt(q_scr[...], kc_ref[0, 0, 0].astype(BF16)), vc_ref[0, 0, 0].astype(BF16))
        o = acc[...] / l_scr[...]
        for g in range(group):
            o_ref[rows, g * hd:(g + 1) * hd] = o[g * tq:(g + 1) * tq].astype(BF16)
        return c

    lax.fori_loop(0, seq // tq, query_tile, 0)


def _attention(proj, seq_len, n_kv, group, q_col, k_col, v_col, q_gain, k_gain, *,
               rope_tabs=None, cache=None, emit_kv=False, name):
    n = proj.shape[0]
    nseq = n // seq_len
    hd = HEAD_DIM
    tq, tk = min(ATT_TQ, seq_len), min(ATT_TK, seq_len)
    gw = group * hd
    in_specs = [pl.BlockSpec((seq_len, gw), lambda b, h: (b, q_col // gw + h)),
                pl.BlockSpec((seq_len, hd), lambda b, h: (b, k_col // hd + h)),
                pl.BlockSpec((seq_len, hd), lambda b, h: (b, v_col // hd + h)),
                pl.BlockSpec((1, hd), lambda b, h: (0, 0)),
                pl.BlockSpec((1, hd), lambda b, h: (0, 0))]
    args = [proj, proj, proj, q_gain.reshape(1, hd), k_gain.reshape(1, hd)]
    if rope_tabs is not None:
        in_specs += [pl.BlockSpec((seq_len, hd), lambda b, h: (0, 0))] * 3
        args += list(rope_tabs)
    cache_len = 0
    if cache is not None:
        kc, vc, layer = cache
        cache_len = kc.shape[3]
        in_specs += [pl.BlockSpec((1, 1, 1, cache_len, hd), lambda b, h: (b, layer, h, 0, 0))] * 2
        args += [kc, vc]
    out_shape = [jax.ShapeDtypeStruct((n, n_kv * gw), BF16)]
    out_specs = [pl.BlockSpec((seq_len, gw), lambda b, h: (b, h))]
    if emit_kv:
        out_shape += [jax.ShapeDtypeStruct((nseq, n_kv, seq_len, hd), F32)] * 2
        out_specs += [pl.BlockSpec((1, 1, seq_len, hd), lambda b, h: (b, h, 0, 0))] * 2
    m = group * tq
    res = pl.pallas_call(
        functools.partial(_attn_kernel, group=group, seq=seq_len, tq=tq, tk=tk,
                          rope=rope_tabs is not None, cache_len=cache_len, emit_kv=emit_kv),
        out_shape=out_shape,
        grid=(nseq, n_kv),
        in_specs=in_specs,
        out_specs=out_specs,
        scratch_shapes=[pltpu.VMEM((seq_len, hd), BF16), pltpu.VMEM((seq_len, hd), BF16),
                        pltpu.VMEM((m, hd), F32), pltpu.VMEM((m, hd), F32), pltpu.VMEM((m, hd), F32),
                        pltpu.VMEM((m, hd), BF16)],
        compiler_params=_cparams(2),
        name=name,
    )(*args)
    return res if emit_kv else res[0]


def _natten_kernel(q_ref, k_ref, v_ref, qn_ref, kn_ref, kc_ref, vc_ref, bias_ref, o_ref, kb, vb, *,
                   seq, width, kr, chunk):
    n_rows = seq // width
    scale = HEAD_DIM ** -0.5

    def key_chunk(i, c):
        rows = _rows(i, chunk)
        kb[rows, :] = _rms(k_ref[rows, :], kn_ref[...]).astype(BF16)
        vb[rows, :] = v_ref[rows, :].astype(BF16)
        return c

    lax.fori_loop(0, seq // chunk, key_chunk, 0)

    rg = NAT_ROWS if n_rows % NAT_ROWS == 0 else 1

    def row_group(gi, c):
        qrows = _rows(gi, rg * width)
        q = (_rms(q_ref[qrows, :], qn_ref[...]) * scale).astype(BF16)
        s_ctx = _dot_nt(q, kc_ref[0, 0, 0].astype(BF16))
        krows, s_loc = [], []
        for j in range(rg):
            r = gi * rg + j
            r0 = jnp.clip(r - kr // 2, 0, n_rows - kr)
            krows.append(pl.ds(pl.multiple_of(r0 * width, width), kr * width))
            s_loc.append(_dot_nt(q[j * width:(j + 1) * width], kb[krows[j], :]) + bias_ref[0, r - r0])
        s_loc = jnp.concatenate(s_loc, axis=0) if rg > 1 else s_loc[0]
        m = jnp.maximum(jnp.max(s_loc, axis=-1, keepdims=True), jnp.max(s_ctx, axis=-1, keepdims=True))
        p_loc = jnp.exp(s_loc - m)
        p_ctx = jnp.exp(s_ctx - m)
        denom = jnp.sum(p_loc, axis=-1, keepdims=True) + jnp.sum(p_ctx, axis=-1, keepdims=True)
        p_loc = p_loc.astype(BF16)
        o_loc = [_dot(p_loc[j * width:(j + 1) * width], vb[krows[j], :]) for j in range(rg)]
        o_loc = jnp.concatenate(o_loc, axis=0) if rg > 1 else o_loc[0]
        o = o_loc + _dot(p_ctx.astype(BF16), vc_ref[0, 0, 0].astype(BF16))
        o_ref[qrows, :] = (o / denom).astype(BF16)
        return c

    lax.fori_loop(0, n_rows // rg, row_group, 0)


def _natten_bias(rpb, width, kr):
    kc = NAT_KC
    cols = np.arange(width)
    c0 = np.clip(cols - kc // 2, 0, width - kc)
    j = np.arange(width)
    in_win = (j[None, :] >= c0[:, None]) & (j[None, :] < c0[:, None] + kc)
    ci = np.clip(j[None, :] - cols[:, None] + kc - 1, 0, 2 * kc - 2)
    off = np.arange(kr)
    ri = np.arange(kr)[None, :] - off[:, None] + NAT_KR - 1
    pick_r = np.zeros((kr * kr, 2 * NAT_KR - 1), np.float32)
    pick_r[np.arange(kr * kr), ri.reshape(-1)] = 1.0
    pick_c = np.zeros((2 * kc - 1, width * width), np.float32)
    pick_c[ci.reshape(-1), np.arange(width * width)] = 1.0
    tab = jnp.einsum('ar,hrs,sb->hab', pick_r, rpb.astype(F32), pick_c, precision=lax.Precision.HIGHEST)
    tab = tab.reshape(rpb.shape[0], kr, kr, width, width).transpose(0, 1, 3, 2, 4)
    tab = jnp.where(in_win[None, None, :, None, :], tab, MASKED)
    return tab.reshape(rpb.shape[0], kr, width, kr * width)


def _natten(proj, seq_len, q_gain, k_gain, rpb, kc, vc, layer):
    n = proj.shape[0]
    nseq = n // seq_len
    hd = HEAD_DIM
    n_heads = rpb.shape[0]
    width = GRID_W
    kr = min(NAT_KR, seq_len // width)
    bias = _natten_bias(rpb, width, kr)
    cache_len = kc.shape[3]
    chunk = min(512, seq_len)
    col = lambda c0: pl.BlockSpec((seq_len, hd), lambda b, h: (b, c0 // hd + h))
    return pl.pallas_call(
        functools.partial(_natten_kernel, seq=seq_len, width=width, kr=kr, chunk=chunk),
        out_shape=jax.ShapeDtypeStruct((n, n_heads * hd), BF16),
        grid=(nseq, n_heads),
        in_specs=[col(COL_NQ), col(COL_NK), col(COL_NV),
                  pl.BlockSpec((1, hd), lambda b, h: (0, 0)),
                  pl.BlockSpec((1, hd), lambda b, h: (0, 0)),
                  pl.BlockSpec((1, 1, 1, cache_len, hd), lambda b, h: (b, layer, h, 0, 0)),
                  pl.BlockSpec((1, 1, 1, cache_len, hd), lambda b, h: (b, layer, h, 0, 0)),
                  pl.BlockSpec((1, kr, width, kr * width), lambda b, h: (h, 0, 0, 0))],
        out_specs=pl.BlockSpec((seq_len, hd), lambda b, h: (b, h)),
        scratch_shapes=[pltpu.VMEM((seq_len, hd), BF16), pltpu.VMEM((seq_len, hd), BF16)],
        compiler_params=_cparams(2),
        name="natten_latent",
    )(proj, proj, proj, q_gain.reshape(1, hd), k_gain.reshape(1, hd), kc, vc, bias)


def _retention_kernel(q_ref, k_ref, v_ref, rg_ref, lg_ref, gn_ref, sf0_ref, sb0_ref,
                      y_ref, sf_ref, sb_ref, o_scr, kv_scr, sp_scr, *, seq, unroll):
    c = RET_CHUNK
    nc = seq // c
    hd = HEAD_DIM
    u = unroll
    lgf = lg_ref[0, 0:1, :]
    lgb = lg_ref[0, 1:2, :]
    ii = lax.broadcasted_iota(jnp.int32, (c, c), 0)
    jj = lax.broadcasted_iota(jnp.int32, (c, c), 1)
    rel = (ii - jj).astype(F32)
    pos = lax.broadcasted_iota(jnp.int32, (c, hd), 0).astype(F32)
    decay = (jnp.where(rel >= 0, jnp.exp(lgf * jnp.maximum(rel, 0.0)), 0.0)
             + jnp.where(rel <= 0, jnp.exp(lgb * jnp.maximum(-rel, 0.0)), 0.0))
    qd = jnp.concatenate([jnp.exp(lgf * (pos + 1.0)), jnp.exp(lgb * (c - pos))], axis=1)
    kd = jnp.concatenate([jnp.exp(lgf * (c - 1.0 - pos)), jnp.exp(lgb * pos)], axis=1)
    cd_f = jnp.exp(lgf * c)
    cd_b = jnp.exp(lgb * c)
    kscale = HEAD_DIM ** -0.5

    def within_chunks(gi, carry):
        rows = [_rows(gi * u + j, c) for j in range(u)]
        k = [k_ref[r, :] * kscale for r in rows]
        v = [v_ref[r, :].astype(BF16) for r in rows]
        qk = [_dot_nt(q_ref[rows[j], :].astype(BF16), k[j].astype(BF16)) for j in range(u)]
        att = [(s * decay).astype(BF16) for s in qk]
        for j in range(u):
            o_scr[rows[j], :] = _dot(att[j], v[j])
        for j in range(u):
            kk = jnp.concatenate([k[j], k[j]], axis=1) * kd
            kv_scr[gi * u + j] = _dot(kk.T.astype(BF16), v[j])
        return carry

    lax.fori_loop(0, nc // u, within_chunks, 0)

    def recurrence(t, carry):
        sf, sb = carry
        tb = nc - 1 - t
        sp_scr[t, 0:hd, :] = sf.astype(BF16)
        sp_scr[tb, hd:2 * hd, :] = sb.astype(BF16)
        return sf * cd_f + kv_scr[t, 0:hd, :], sb * cd_b + kv_scr[tb, hd:2 * hd, :]

    sf, sb = lax.fori_loop(0, nc, recurrence, (sf0_ref[0, 0, 0], sb0_ref[0, 0, 0]))
    sf_ref[0, 0] = sf
    sb_ref[0, 0] = sb

    def across_chunks(gi, carry):
        rows = [_rows(gi * u + j, c) for j in range(u)]
        o = []
        for j in range(u):
            q = q_ref[rows[j], :]
            qq = (jnp.concatenate([q, q], axis=1) * qd).astype(BF16)
            o.append(o_scr[rows[j], :] + _dot(qq, sp_scr[gi * u + j]))
        for j in range(u):
            on = o[j] * lax.rsqrt(jnp.mean(o[j] * o[j], axis=-1, keepdims=True) + EPS)
            g = rg_ref[rows[j], :]
            y_ref[rows[j], :] = ((g * _sigmoid(g)) * (on * gn_ref[...])).astype(BF16)
        return carry

    lax.fori_loop(0, nc // u, across_chunks, 0)


def _retention(proj, seq_len, log_gamma, ret_gn, s_fwd, s_bwd, layer):
    n = proj.shape[0]
    nseq = n // seq_len
    hd = HEAD_DIM
    n_heads = log_gamma.shape[0]
    nc = seq_len // RET_CHUNK
    shared = s_fwd.shape[0] == 1
    col = lambda c0: pl.BlockSpec((seq_len, hd), lambda b, h: (b, c0 // hd + h))
    state = pl.BlockSpec((1, 1, 1, hd, hd), lambda b, h: (0 if shared else b, layer, 0 if shared else h, 0, 0))
    return pl.pallas_call(
        functools.partial(_retention_kernel, seq=seq_len, unroll=next(u for u in (RET_UNROLL, 2, 1) if nc % u == 0)),
        out_shape=[jax.ShapeDtypeStruct((n, n_heads * hd), BF16),
                   jax.ShapeDtypeStruct((nseq, n_heads, hd, hd), F32),
                   jax.ShapeDtypeStruct((nseq, n_heads, hd, hd), F32)],
        grid=(nseq, n_heads),
        in_specs=[col(COL_RQ), col(COL_RK), col(COL_RV), col(COL_RG),
                  pl.BlockSpec((1, 2, hd), lambda b, h: (h, 0, 0)),
                  pl.BlockSpec((1, hd), lambda b, h: (0, h)),
                  state, state],
        out_specs=[pl.BlockSpec((seq_len, hd), lambda b, h: (b, h)),
                   pl.BlockSpec((1, 1, hd, hd), lambda b, h: (b, h, 0, 0)),
                   pl.BlockSpec((1, 1, hd, hd), lambda b, h: (b, h, 0, 0))],
        scratch_shapes=[pltpu.VMEM((seq_len, hd), F32), pltpu.VMEM((nc, 2 * hd, hd), F32),
                        pltpu.VMEM((nc, 2 * hd, hd), BF16)],
        compiler_params=_cparams(2),
        name="retention",
    )(proj, proj, proj, proj, log_gamma, ret_gn.reshape(1, n_heads * hd), s_fwd, s_bwd)


def _merge_kernel(y0_ref, y1_ref, y2_ref, y3_ref, g0_ref, g1_ref, g2_ref, g3_ref, wb_ref, wo_ref, o_ref):
    k = pl.program_id(1)
    m = g0_ref[...] * _dot(y0_ref[...], wb_ref[0])
    for i, (y_ref, g_ref) in enumerate(((y1_ref, g1_ref), (y2_ref, g2_ref), (y3_ref, g3_ref)), start=1):
        m = m + g_ref[...] * _dot(y_ref[...], wb_ref[i])
    part = _dot(m.astype(BF16), wo_ref[...])

    @pl.when(k == 0)
    def _():
        o_ref[...] = part

    @pl.when(k > 0)
    def _():
        o_ref[...] += part


def _merge(branches, gates, w_branch, w_out):
    n = gates.shape[0]
    _, bw, d = w_branch.shape
    tm, tk = min(MERGE_TM, n), min(MERGE_TK, d)
    nk = d // tk
    gate = lambda i: pl.BlockSpec((tm, tk), lambda t, k: (t, i * nk + k))
    return pl.pallas_call(
        _merge_kernel,
        out_shape=jax.ShapeDtypeStruct((n, d), F32),
        grid=(n // tm, nk),
        in_specs=[pl.BlockSpec((tm, bw), lambda t, k: (t, 0))] * 4
                 + [gate(0), gate(1), gate(2), gate(3),
                    pl.BlockSpec((N_BRANCH, bw, tk), lambda t, k: (0, 0, k)),
                    pl.BlockSpec((tk, d), lambda t, k: (k, 0))],
        out_specs=pl.BlockSpec((tm, d), lambda t, k: (t, 0)),
        compiler_params=_cparams(2),
        name="merge_out",
    )(*branches, gates, gates, gates, gates, w_branch, w_out)


def _router_kernel(x_ref, a_ref, mod_ref, g_ref, wt_ref, b_ref, x1_ref, h_ref, eid_ref, wts_ref):
    x1 = x_ref[...] + mod_ref[0, 2:3, :] * a_ref[...]
    x1_ref[...] = x1
    y = _rms(x1, g_ref[...])
    h = y * (1.0 + mod_ref[0, 4:5, :]) + mod_ref[0, 3:4, :]
    h_ref[...] = h
    h_hi = h.astype(BF16)
    h_lo = (h - h_hi.astype(F32)).astype(BF16)
    w = wt_ref[...]
    w_hi = w.astype(BF16)
    w_lo = (w - w_hi.astype(F32)).astype(BF16)
    logits = _dot_nt(w_hi, h_hi) + (_dot_nt(w_lo, h_hi) + _dot_nt(w_hi, h_lo))
    score = _sigmoid(logits)
    sel = score + b_ref[...]
    epg = N_EXPERTS // N_GROUPS
    s = [sel[e:e + 1, :] for e in range(N_EXPERTS)]
    sc = [score[e:e + 1, :] for e in range(N_EXPERTS)]

    def group_score(vals):
        best = None
        for a in range(len(vals)):
            for b in range(a + 1, len(vals)):
                pair = vals[a] + vals[b]
                best = pair if best is None else jnp.maximum(best, pair)
        return best

    gs = [group_score(s[g * epg:(g + 1) * epg]) for g in range(N_GROUPS)]
    g_best = jnp.zeros(gs[0].shape, jnp.int32)
    best = gs[0]
    for g in range(1, N_GROUPS):
        better = gs[g] > best
        g_best = jnp.where(better, g, g_best)
        best = jnp.where(better, gs[g], best)
    in_sel, in_score = [], []
    for k in range(epg):
        v, w_ = s[k], sc[k]
        for g in range(1, N_GROUPS):
            v = jnp.where(g_best == g, s[g * epg + k], v)
            w_ = jnp.where(g_best == g, sc[g * epg + k], w_)
        in_sel.append(v)
        in_score.append(w_)
    i1 = jnp.zeros(g_best.shape, jnp.int32)
    v1, w1 = in_sel[0], in_score[0]
    for k in range(1, epg):
        better = in_sel[k] > v1
        i1 = jnp.where(better, k, i1)
        v1 = jnp.where(better, in_sel[k], v1)
        w1 = jnp.where(better, in_score[k], w1)
    i2 = jnp.zeros(g_best.shape, jnp.int32)
    v2 = jnp.full(v1.shape, -jnp.inf, F32)
    w2 = jnp.zeros(v1.shape, F32)
    for k in range(epg):
        better = (i1 != k) & (in_sel[k] > v2)
        i2 = jnp.where(better, k, i2)
        v2 = jnp.where(better, in_sel[k], v2)
        w2 = jnp.where(better, in_score[k], w2)
    total = w1 + w2
    eid_ref[...] = jnp.concatenate([g_best * epg + i1, g_best * epg + i2], axis=0)
    wts_ref[...] = jnp.concatenate([w1 / total, w2 / total], axis=0)


def _router(x, mixed, mods, mod_row0, tiles_per_seq, gain, w_router_t, b_router):
    n, d = x.shape
    tm = min(ROUTER_TM, n)
    tps = max(tiles_per_seq // tm, 1) if tiles_per_seq else n // tm
    e = w_router_t.shape[0]
    tile = pl.BlockSpec((tm, d), lambda i: (i, 0))
    return pl.pallas_call(
        _router_kernel,
        out_shape=[jax.ShapeDtypeStruct((n, d), F32),
                   jax.ShapeDtypeStruct((n, d), F32),
                   jax.ShapeDtypeStruct((TOP_K, n), jnp.int32),
                   jax.ShapeDtypeStruct((TOP_K, n), F32)],
        grid=(n // tm,),
        in_specs=[tile, tile,
                  pl.BlockSpec((1, 6, d), lambda i: (mod_row0 + i // tps, 0, 0)),
                  pl.BlockSpec((1, d), lambda i: (0, 0)),
                  pl.BlockSpec((e, d), lambda i: (0, 0)),
                  pl.BlockSpec((e, 1), lambda i: (0, 0))],
        out_specs=[tile, tile,
                   pl.BlockSpec((TOP_K, tm), lambda i: (0, i)),
                   pl.BlockSpec((TOP_K, tm), lambda i: (0, i))],
        compiler_params=_cparams(1),
        name="router",
    )(x, mixed, mods, gain.reshape(1, d), w_router_t, b_router.reshape(e, 1))


def _row_copy(src_ref, src_row, dst_ref, dst_row, sem):
    return pltpu.make_async_copy(src_ref.at[pl.ds(src_row, 1)], dst_ref.at[pl.ds(dst_row, 1)], sem)


def _dispatch_kernel(slot_ref, h_ref, buf_in_ref, buf_ref, sem, *, tt):
    del buf_in_ref

    def copies(j):
        return [_row_copy(h_ref, j, buf_ref, slot_ref[0, 0, k * tt + j], sem) for k in range(TOP_K)]

    def start(j, c):
        for k, cp in enumerate(copies(j)):
            cp.start(priority=k % 2)
        return c

    def wait(j, c):
        for cp in copies(j):
            cp.wait()
        return c

    lax.fori_loop(0, tt, start, 0, unroll=DMA_UNROLL)
    lax.fori_loop(0, tt, wait, 0, unroll=DMA_UNROLL)


def _dispatch(h, slots, buf):
    n, d = h.shape
    tt = min(MOE_TT, n)
    return pl.pallas_call(
        functools.partial(_dispatch_kernel, tt=tt),
        out_shape=jax.ShapeDtypeStruct(buf.shape, buf.dtype),
        grid=(n // tt,),
        in_specs=[pl.BlockSpec((1, 1, TOP_K * tt), lambda i: (i, 0, 0), memory_space=pltpu.SMEM),
                  pl.BlockSpec((tt, d), lambda i: (i, 0)),
                  pl.BlockSpec(memory_space=pl.ANY)],
        out_specs=pl.BlockSpec(memory_space=pl.ANY),
        scratch_shapes=[pltpu.SemaphoreType.DMA],
        input_output_aliases={2: 0},
        compiler_params=_cparams(1),
        name="moe_dispatch",
    )(slots, h, buf)


def _ffn_kernel(be_ref, nu_ref, x_ref, wg_ref, wu_ref, wd_ref, o_ref):
    del be_ref
    live = pl.program_id(0) < nu_ref[0]

    @pl.when(live)
    def _():
        x = x_ref[...].astype(BF16)
        a = _dot(x, wg_ref[0])
        b = _dot(x, wu_ref[0])
        o_ref[...] = _dot(((a * _sigmoid(a)) * b).astype(BF16), wd_ref[0])

    @pl.when(jnp.logical_not(live))
    def _():
        o_ref[...] = jnp.zeros(o_ref.shape, F32)


def _expert_ffn(buf, block_expert, n_used, w_gate, w_up, w_down):
    rows, d = buf.shape
    de = w_gate.shape[2]
    rb = MOE_ROWS
    grid_spec = pltpu.PrefetchScalarGridSpec(
        num_scalar_prefetch=2,
        grid=(rows // rb,),
        in_specs=[pl.BlockSpec((rb, d), lambda i, be, nu: (i, 0)),
                  pl.BlockSpec((1, d, de), lambda i, be, nu: (be[i], 0, 0)),
                  pl.BlockSpec((1, d, de), lambda i, be, nu: (be[i], 0, 0)),
                  pl.BlockSpec((1, de, d), lambda i, be, nu: (be[i], 0, 0))],
        out_specs=pl.BlockSpec((rb, d), lambda i, be, nu: (i, 0)))
    return pl.pallas_call(
        _ffn_kernel,
        out_shape=jax.ShapeDtypeStruct((rows, d), F32),
        grid_spec=grid_spec,
        compiler_params=_cparams(1),
        name="expert_ffn",
    )(block_expert, n_used, buf, w_gate, w_up, w_down)


def _combine_kernel(slot_ref, yb_ref, x_ref, w_ref, mod_ref, o_ref, g0, g1, sem, *, tt):
    bufs = (g0, g1)

    def copies(j):
        return [_row_copy(yb_ref, slot_ref[0, 0, k * tt + j], bufs[k], j, sem) for k in range(TOP_K)]

    def start(j, c):
        for k, cp in enumerate(copies(j)):
            cp.start(priority=k % 2)
        return c

    def wait(j, c):
        for cp in copies(j):
            cp.wait()
        return c

    lax.fori_loop(0, tt, start, 0, unroll=DMA_UNROLL)
    lax.fori_loop(0, tt, wait, 0, unroll=DMA_UNROLL)
    w = w_ref[...]
    y = w[:, 0:1] * g0[...] + w[:, 1:2] * g1[...]
    o_ref[...] = x_ref[...] + mod_ref[0, 5:6, :] * y


def _combine(x, yb, slots, wts, mods, mod_row0, tiles_per_seq):
    n, d = x.shape
    tt = min(MOE_TT, n)
    tps = max(tiles_per_seq // tt, 1) if tiles_per_seq else n // tt
    return pl.pallas_call(
        functools.partial(_combine_kernel, tt=tt),
        out_shape=jax.ShapeDtypeStruct((n, d), F32),
        grid=(n // tt,),
        in_specs=[pl.BlockSpec((1, 1, TOP_K * tt), lambda i: (i, 0, 0), memory_space=pltpu.SMEM),
                  pl.BlockSpec(memory_space=pl.ANY),
                  pl.BlockSpec((tt, d), lambda i: (i, 0)),
                  pl.BlockSpec((tt, TOP_K), lambda i: (i, 0)),
                  pl.BlockSpec((1, 6, d), lambda i: (mod_row0 + i // tps, 0, 0))],
        out_specs=pl.BlockSpec((tt, d), lambda i: (i, 0)),
        scratch_shapes=[pltpu.VMEM((tt, d), F32), pltpu.VMEM((tt, d), F32), pltpu.SemaphoreType.DMA],
        compiler_params=_cparams(1),
        name="moe_combine",
    )(slots, yb, x, wts, mods)


def _slot_blocks(slots, tt):
    k, n = slots.shape
    return slots.reshape(k, n // tt, tt).transpose(1, 0, 2).reshape(n // tt, 1, k * tt)


def _moe(xs, mixed, mods, mod_rows, seq_lens, gain, w_router, b_router, w_gate, w_up, w_down, spare=None):
    d = xs[0].shape[1]
    w_router_t = w_router.T
    routed = [_router(x, a, mods, r0, sl, gain, w_router_t, b_router)
              for x, a, r0, sl in zip(xs, mixed, mod_rows, seq_lens)]
    flat_e = jnp.concatenate([r[2].reshape(-1) for r in routed])
    n_assign = flat_e.shape[0]
    onehot = (flat_e[:, None] == jnp.arange(N_EXPERTS, dtype=jnp.int32)[None, :]).astype(jnp.int32)
    csum = jnp.cumsum(onehot, axis=0)
    counts = csum[-1]
    padded = (counts + MOE_ROWS - 1) // MOE_ROWS * MOE_ROWS
    pend = jnp.cumsum(padded)
    slot = jnp.sum(onehot * (csum - 1 + (pend - padded)[None, :]), axis=1)
    n_blocks = -(-n_assign // MOE_ROWS) + N_EXPERTS
    if spare is not None and spare.size % (MOE_ROWS * d) == 0 and spare.size // (MOE_ROWS * d) >= n_blocks:
        buf = spare.reshape(-1, d)
        n_blocks = buf.shape[0] // MOE_ROWS
    else:
        buf = jnp.zeros((n_blocks * MOE_ROWS, d), F32)
    block_start = jnp.arange(n_blocks, dtype=jnp.int32) * MOE_ROWS
    block_expert = jnp.minimum(jnp.sum((pend[None, :] <= block_start[:, None]).astype(jnp.int32), axis=1),
                               N_EXPERTS - 1)
    n_used = (pend[-1:] // MOE_ROWS).astype(jnp.int32)
    slot_blocks, off = [], 0
    for r in routed:
        n = r[0].shape[0]
        sb = _slot_blocks(slot[off:off + TOP_K * n].reshape(TOP_K, n), min(MOE_TT, n))
        off += TOP_K * n
        slot_blocks.append(sb)
        buf = _dispatch(r[1], sb, buf)
    yb = _expert_ffn(buf, block_expert, n_used, w_gate, w_up, w_down)
    return [_combine(r[0], yb, sb, r[3].T, mods, r0, sl)
            for r, sb, r0, sl in zip(routed, slot_blocks, mod_rows, seq_lens)]


def _rope_tables(n_tok):
    t = jnp.arange(n_tok)
    row = (t // GRID_W).astype(F32)
    col = (t % GRID_W).astype(F32)
    quarter = HEAD_DIM // 4
    inv = ROPE_THETA ** (-jnp.arange(quarter, dtype=F32) / quarter)
    ar = row[:, None] * inv
    ac = col[:, None] * inv
    ang = jnp.concatenate([ar, ar, ac, ac], axis=-1)
    cos, sin = jnp.cos(ang), jnp.sin(ang)
    first = (jnp.arange(HEAD_DIM) % (2 * quarter)) < quarter
    return cos, jnp.where(first, -sin, 0.0), jnp.where(first, 0.0, sin)


def kernel(x_prompt, x_sample, cache_nat_k, cache_nat_v, cache_gqa_k, cache_gqa_v, state_ret_fwd,
           state_ret_bwd, c, c_ctx, w_mod, b_mod, norm1, norm2, w_in, conv_w, nat_qn, nat_kn, nat_rpb,
           gqa_qn, gqa_kn, ret_decay_fwd, ret_decay_bwd, ret_gn, w_branch, w_out, w_router, b_router,
           w_exp_gate, w_exp_up, w_exp_down):
    batch, seq, d = x_prompt.shape
    dec_batch, dec_seq, _ = x_sample.shape
    depth = w_mod.shape[0]
    n_heads = nat_rpb.shape[1]
    n_kv = cache_gqa_k.shape[2]
    hd = HEAD_DIM

    xp = x_prompt.reshape(batch * seq, d)
    xs = x_sample.reshape(dec_batch * dec_seq, d)
    cvecs = jnp.zeros((SUBLANES, d), F32).at[0].set(c_ctx).at[1:1 + dec_batch].set(c)
    mods_all = _modulation(cvecs, w_mod, b_mod).reshape(depth, SUBLANES, 6, d)
    rope_tabs = _rope_tables(dec_seq)
    zero_state = jnp.zeros((1, depth, 1, hd, hd), F32)

    caches = [[] for _ in range(6)]
    for l in range(depth):
        mods = mods_all[l]
        w_proj = w_in[l][:, :PROJ_W].astype(BF16)
        w_gatel = w_in[l][:, PROJ_W:].astype(BF16)
        wb = w_branch[l].astype(BF16)
        wo = w_out[l].astype(BF16)
        lg = jnp.stack([jax.nn.log_sigmoid(ret_decay_fwd[l].astype(F32)),
                        jax.nn.log_sigmoid(ret_decay_bwd[l].astype(F32))], axis=1)
        lg = jnp.broadcast_to(lg[:, :, None], (lg.shape[0], 2, hd))

        pc = _norm_project(xp, mods, 0, 0, norm1[l], w_proj, gate=False, name="in_proj_ctx")
        gc = _norm_project(xp, mods, 0, 0, norm1[l], w_gatel, gate=True, name="gate_proj_ctx")
        conv_c = _short_conv(pc, conv_w[l], seq)
        nat_c, nk, nv = _attention(pc, seq, n_heads, 1, COL_NQ, COL_NK, COL_NV, nat_qn[l], nat_kn[l],
                                   emit_kv=True, name="nat_ctx")
        gqa_c, gk, gv = _attention(pc, seq, n_kv, GQA_GROUP, COL_GQ, COL_GK, COL_GV, gqa_qn[l], gqa_kn[l],
                                   emit_kv=True, name="gqa_ctx")
        ret_c, s_f, s_b = _retention(pc, seq, lg, ret_gn[l], zero_state, zero_state, l)
        for lst, val in zip(caches, (nk, nv, gk, gv, s_f, s_b)):
            lst.append(val)
        mix_c = _merge((conv_c, nat_c, gqa_c, ret_c), gc, wb, wo)

        pl_ = _norm_project(xs, mods, 1, dec_seq, norm1[l], w_proj, gate=False, name="in_proj_lat")
        gl = _norm_project(xs, mods, 1, dec_seq, norm1[l], w_gatel, gate=True, name="gate_proj_lat")
        conv_l = _short_conv(pl_, conv_w[l], dec_seq)
        nat_l = _natten(pl_, dec_seq, nat_qn[l], nat_kn[l], nat_rpb[l], cache_nat_k, cache_nat_v, l)
        gqa_l = _attention(pl_, dec_seq, n_kv, GQA_GROUP, COL_GQ, COL_GK, COL_GV, gqa_qn[l], gqa_kn[l],
                           rope_tabs=rope_tabs, cache=(cache_gqa_k, cache_gqa_v, l), name="gqa_lat")
        ret_l, _, _ = _retention(pl_, dec_seq, lg, ret_gn[l], state_ret_fwd, state_ret_bwd, l)
        mix_l = _merge((conv_l, nat_l, gqa_l, ret_l), gl, wb, wo)

        xp, xs = _moe([xp, xs], [mix_c, mix_l], mods, [0, 1], [0, dec_seq], norm2[l], w_router, b_router,
                      w_exp_gate[l].astype(BF16), w_exp_up[l].astype(BF16), w_exp_down[l].astype(BF16),
                      spare=pl_)

    outs = [jnp.stack(v, axis=1) for v in caches]
    return (xp.reshape(batch, seq, d), xs.reshape(dec_batch, dec_seq, d), *outs)
```

```python
import functools

import numpy as np
import jax
import jax.numpy as jnp
from jax import lax
from jax.experimental import pallas as pl
from jax.experimental.pallas import tpu as pltpu

F32 = jnp.float32
BF16 = jnp.bfloat16

GRID_W = 64
HEAD_DIM = 128
BRANCH_W = 512
N_BRANCH = 4
CONV_K = 3
NAT_KR = 8
NAT_KC = 16
GQA_GROUP = 2
RET_CHUNK = 128
ROPE_THETA = 10000.0
N_EXPERTS = 16
N_GROUPS = 4
TOP_K = 2
EPS = 1e-6
MASKED = -1e30
LOG2_E = 1.4426950408889634

COL_U, COL_BG, COL_CG = 0, 512, 1024
COL_NQ, COL_NK, COL_NV = 1536, 2048, 2560
COL_GQ, COL_GK, COL_GV = 3072, 3584, 3840
COL_RQ, COL_RK, COL_RV, COL_RG = 4096, 4608, 5120, 5632
PROJ_W = 6144

V7X_VMEM_LIMIT_BYTES = 56 * 1024 * 1024
SUBLANES = 8
LANES = 128

PROJ_TM, PROJ_TN = 1024, 1024
NORM_ROWS = 256
MERGE_TM, MERGE_TK = 1024, 512
ROUTER_TM = 512
MOE_ROWS = 256
MOE_TT = 256
CONV_TT = 1024
ATT_TQ, ATT_TK = 512, 1024
MOD_TN = 1024
NAT_ROWS = 16
RET_UNROLL = 4
DMA_UNROLL = 8


def _cparams(n_axes):
    return pltpu.CompilerParams(dimension_semantics=("arbitrary",) * n_axes,
                                vmem_limit_bytes=V7X_VMEM_LIMIT_BYTES)


def _sigmoid(x):
    return 1.0 / (1.0 + jnp.exp(-x))


def _dot(a, b):
    return jnp.dot(a, b, preferred_element_type=F32)


def _dot_nt(a, b):
    return lax.dot_general(a, b, (((1,), (1,)), ((), ())), preferred_element_type=F32)


def _rms(x, gain):
    return x * lax.rsqrt(jnp.mean(x * x, axis=-1, keepdims=True) + EPS) * gain


def _rows(i, n):
    return pl.ds(pl.multiple_of(i * n, n), n)


def _mod_kernel(c_ref, w_ref, b_ref, o_ref):
    c = c_ref[...]
    a = (c * _sigmoid(c)).astype(BF16)
    o_ref[0] = _dot(a, w_ref[0].astype(BF16)) + b_ref[0]


def _modulation(cvecs, w_mod, b_mod):
    depth, d, n = w_mod.shape
    tn = min(MOD_TN, n)
    return pl.pallas_call(
        _mod_kernel,
        out_shape=jax.ShapeDtypeStruct((depth, cvecs.shape[0], n), F32),
        grid=(depth, n // tn),
        in_specs=[pl.BlockSpec(cvecs.shape, lambda l, j: (0, 0)),
                  pl.BlockSpec((1, d, tn), lambda l, j: (l, 0, j)),
                  pl.BlockSpec((1, 1, tn), lambda l, j: (l, 0, j))],
        out_specs=pl.BlockSpec((1, cvecs.shape[0], tn), lambda l, j: (l, 0, j)),
        compiler_params=_cparams(2),
        name="modulation",
    )(cvecs, w_mod, b_mod.reshape(depth, 1, n))


def _normproj_kernel(x_ref, mod_ref, g_ref, w_ref, o_ref, h_scr, *, gate):
    @pl.when(pl.program_id(1) == 0)
    def _():
        gain = g_ref[...] * (1.0 + mod_ref[0, 1:2, :])
        shift = mod_ref[0, 0:1, :]

        def chunk(i, c):
            rows = _rows(i, NORM_ROWS)
            x = x_ref[rows, :]
            y = x * lax.rsqrt(jnp.mean(x * x, axis=-1, keepdims=True) + EPS)
            h_scr[rows, :] = (y * gain + shift).astype(BF16)
            return c

        lax.fori_loop(0, x_ref.shape[0] // NORM_ROWS, chunk, 0)

    acc = _dot(h_scr[...], w_ref[0].astype(BF16))
    o_ref[...] = _sigmoid(acc) if gate else acc


def _norm_project(x, mods, mod_row0, tiles_per_seq, gain, w_all, layer, col0, nout, *, gate, name):
    n, d = x.shape
    tm, tn = min(PROJ_TM, n), min(PROJ_TN, nout)
    assert col0 % tn == 0 and nout % tn == 0
    tps = max(tiles_per_seq // tm, 1) if tiles_per_seq else n // tm
    return pl.pallas_call(
        functools.partial(_normproj_kernel, gate=gate),
        out_shape=jax.ShapeDtypeStruct((n, nout), F32),
        grid=(n // tm, nout // tn),
        in_specs=[pl.BlockSpec((tm, d), lambda i, j: (i, 0)),
                  pl.BlockSpec((1, 6, d), lambda i, j: (mod_row0 + i // tps, 0, 0)),
                  pl.BlockSpec((1, d), lambda i, j: (0, 0)),
                  pl.BlockSpec((1, d, tn), lambda i, j: (layer, 0, col0 // tn + j))],
        out_specs=pl.BlockSpec((tm, tn), lambda i, j: (i, j)),
        scratch_shapes=[pltpu.VMEM((tm, d), BF16)],
        compiler_params=_cparams(2),
        name=name,
    )(x, mods, gain.reshape(1, d), w_all)


def _conv_kernel(u_ref, bg_ref, cg_ref, up_ref, cp_ref, un_ref, cn_ref, w_ref, o_ref, *, tt, nt):
    t = pl.program_id(1)
    z = cg_ref[...] * u_ref[...]
    last = SUBLANES - 1
    z_before = jnp.where(t > 0, cp_ref[last:last + 1, :] * up_ref[last:last + 1, :], 0.0)
    z_after = jnp.where(t < nt - 1, cn_ref[0:1, :] * un_ref[0:1, :], 0.0)
    ri = lax.broadcasted_iota(jnp.int32, z.shape, 0)
    z_prev = jnp.where(ri == 0, z_before, pltpu.roll(z, 1, 0))
    z_next = jnp.where(ri == tt - 1, z_after, pltpu.roll(z, tt - 1, 0))
    w = w_ref[...]
    o_ref[...] = (bg_ref[...] * (w[0:1, :] * z_prev + w[1:2, :] * z + w[2:3, :] * z_next)).astype(BF16)


def _short_conv(proj, conv_w, seq_len):
    n = proj.shape[0]
    cw = conv_w.shape[1]
    nseq = n // seq_len
    tt = min(CONV_TT, seq_len)
    nt = seq_len // tt
    cu, cb, cc = COL_U // cw, COL_BG // cw, COL_CG // cw
    nblk8 = n // SUBLANES

    def main(col):
        return pl.BlockSpec((tt, cw), lambda b, t: (b * nt + t, col))

    def before(col):
        return pl.BlockSpec((SUBLANES, cw),
                            lambda b, t: (jnp.maximum((b * nt + t) * (tt // SUBLANES) - 1, 0), col))

    def after(col):
        return pl.BlockSpec((SUBLANES, cw),
                            lambda b, t: (jnp.minimum((b * nt + t + 1) * (tt // SUBLANES), nblk8 - 1), col))

    return pl.pallas_call(
        functools.partial(_conv_kernel, tt=tt, nt=nt),
        out_shape=jax.ShapeDtypeStruct((n, cw), BF16),
        grid=(nseq, nt),
        in_specs=[main(cu), main(cb), main(cc), before(cu), before(cc), after(cu), after(cc),
                  pl.BlockSpec(conv_w.shape, lambda b, t: (0, 0))],
        out_specs=pl.BlockSpec((tt, cw), lambda b, t: (b * nt + t, 0)),
        compiler_params=_cparams(2),
        name="short_conv",
    )(proj, proj, proj, proj, proj, proj, proj, conv_w)


def _attn_kernel(*refs, group, seq, tq, tk, rope, cache_len, emit_kv):
    it = iter(refs)
    q_ref, k_ref, v_ref, qn_ref, kn_ref = (next(it) for _ in range(5))
    if rope:
        cos_ref, sin_lo_ref, sin_hi_ref = (next(it) for _ in range(3))
    if cache_len:
        kc_ref, vc_ref = next(it), next(it)
    o_ref = next(it)
    if emit_kv:
        ko_ref, vo_ref = next(it), next(it)
    kb, vb, m_scr, l_scr, acc, q_scr = (next(it) for _ in range(6))
    scale = HEAD_DIM ** -0.5 * LOG2_E
    hd = HEAD_DIM

    def prep(x, gain_ref, rows):
        y = _rms(x, gain_ref[...])
        if rope:
            y = (y * cos_ref[rows, :] + pltpu.roll(y, hd - hd // 4, 1) * sin_lo_ref[rows, :]
                 + pltpu.roll(y, hd // 4, 1) * sin_hi_ref[rows, :])
        return y

    def key_tile(kt, c):
        rows = _rows(kt, tk)
        kn = prep(k_ref[rows, :], kn_ref, rows)
        v = v_ref[rows, :]
        kb[rows, :] = kn.astype(BF16)
        vb[rows, :] = v.astype(BF16)
        if emit_kv:
            ko_ref[0, 0, rows, :] = kn
            vo_ref[0, 0, rows, :] = v
        return c

    lax.fori_loop(0, seq // tk, key_tile, 0)

    def softmax_step(s, vblk):
        m_prev = m_scr[...]
        m_new = jnp.maximum(m_prev, jnp.max(s, axis=-1, keepdims=True))
        alpha = jnp.exp2(m_prev - m_new)
        p = jnp.exp2(s - jnp.concatenate([m_new] * (s.shape[1] // hd), axis=1))
        l_scr[...] = alpha * l_scr[...] + jnp.sum(p, axis=-1, keepdims=True)
        acc[...] = alpha * acc[...] + _dot(p.astype(BF16), vblk)
        m_scr[...] = m_new

    def query_tile(qt, c):
        rows = _rows(qt, tq)
        for g in range(group):
            q_scr[g * tq:(g + 1) * tq, :] = (
                prep(q_ref[rows, g * hd:(g + 1) * hd], qn_ref, rows) * scale).astype(BF16)
        m_scr[...] = jnp.full(m_scr.shape, MASKED, F32)
        l_scr[...] = jnp.zeros(l_scr.shape, F32)
        acc[...] = jnp.zeros(acc.shape, F32)
        n_kv = seq // tk

        def scores(kt):
            return _dot_nt(q_scr[...], kb[_rows(kt, tk), :])

        def kv_tile(kt, c2):
            softmax_step(scores(kt), vb[_rows(kt, tk), :])
            return c2

        lax.fori_loop(0, n_kv, kv_tile, 0)
        if cache_len:
            softmax_step(_dot_nt(q_scr[...], kc_ref[0, 0, 0].astype(BF16)), vc_ref[0, 0, 0].astype(BF16))
        o = acc[...] / l_scr[...]
        for g in range(group):
            o_ref[rows, g * hd:(g + 1) * hd] = o[g * tq:(g + 1) * tq].astype(BF16)
        return c

    lax.fori_loop(0, seq // tq, query_tile, 0)


def _attention(proj, seq_len, n_kv, group, q_col, k_col, v_col, q_gain, k_gain, *,
               rope_tabs=None, cache=None, emit_kv=False, name):
    n = proj.shape[0]
    nseq = n // seq_len
    hd = HEAD_DIM
    tq, tk = min(ATT_TQ, seq_len), min(ATT_TK, seq_len)
    gw = group * hd
    in_specs = [pl.BlockSpec((seq_len, gw), lambda b, h: (b, q_col // gw + h)),
                pl.BlockSpec((seq_len, hd), lambda b, h: (b, k_col // hd + h)),
                pl.BlockSpec((seq_len, hd), lambda b, h: (b, v_col // hd + h)),
                pl.BlockSpec((1, hd), lambda b, h: (0, 0)),
                pl.BlockSpec((1, hd), lambda b, h: (0, 0))]
    args = [proj, proj, proj, q_gain.reshape(1, hd), k_gain.reshape(1, hd)]
    if rope_tabs is not None:
        in_specs += [pl.BlockSpec((seq_len, hd), lambda b, h: (0, 0))] * 3
        args += list(rope_tabs)
    cache_len = 0
    if cache is not None:
        kc, vc, layer = cache
        cache_len = kc.shape[3]
        in_specs += [pl.BlockSpec((1, 1, 1, cache_len, hd), lambda b, h: (b, layer, h, 0, 0))] * 2
        args += [kc, vc]
    out_shape = [jax.ShapeDtypeStruct((n, n_kv * gw), BF16)]
    out_specs = [pl.BlockSpec((seq_len, gw), lambda b, h: (b, h))]
    if emit_kv:
        out_shape += [jax.ShapeDtypeStruct((nseq, n_kv, seq_len, hd), F32)] * 2
        out_specs += [pl.BlockSpec((1, 1, seq_len, hd), lambda b, h: (b, h, 0, 0))] * 2
    m = group * tq
    res = pl.pallas_call(
        functools.partial(_attn_kernel, group=group, seq=seq_len, tq=tq, tk=tk,
                          rope=rope_tabs is not None, cache_len=cache_len, emit_kv=emit_kv),
        out_shape=out_shape,
        grid=(nseq, n_kv),
        in_specs=in_specs,
        out_specs=out_specs,
        scratch_shapes=[pltpu.VMEM((seq_len, hd), BF16), pltpu.VMEM((seq_len, hd), BF16),
                        pltpu.VMEM((m, hd), F32), pltpu.VMEM((m, hd), F32), pltpu.VMEM((m, hd), F32),
                        pltpu.VMEM((m, hd), BF16)],
        compiler_params=_cparams(2),
        name=name,
    )(*args)
    return res if emit_kv else res[0]


def _natten_kernel(q_ref, k_ref, v_ref, qn_ref, kn_ref, kc_ref, vc_ref, bias_ref, o_ref, kb, vb, *,
                   seq, width, kr, chunk):
    n_rows = seq // width
    scale = HEAD_DIM ** -0.5

    def key_chunk(i, c):
        rows = _rows(i, chunk)
        kb[rows, :] = _rms(k_ref[rows, :], kn_ref[...]).astype(BF16)
        vb[rows, :] = v_ref[rows, :].astype(BF16)
        return c

    lax.fori_loop(0, seq // chunk, key_chunk, 0)

    rg = NAT_ROWS if n_rows % NAT_ROWS == 0 else 1

    def row_group(gi, c):
        qrows = _rows(gi, rg * width)
        q = (_rms(q_ref[qrows, :], qn_ref[...]) * scale).astype(BF16)
        s_ctx = _dot_nt(q, kc_ref[0, 0, 0].astype(BF16))
        krows, s_loc = [], []
        for j in range(rg):
            r = gi * rg + j
            r0 = jnp.clip(r - kr // 2, 0, n_rows - kr)
            krows.append(pl.ds(pl.multiple_of(r0 * width, width), kr * width))
            s_loc.append(_dot_nt(q[j * width:(j + 1) * width], kb[krows[j], :]) + bias_ref[0, r - r0])
        s_loc = jnp.concatenate(s_loc, axis=0) if rg > 1 else s_loc[0]
        m = jnp.maximum(jnp.max(s_loc, axis=-1, keepdims=True), jnp.max(s_ctx, axis=-1, keepdims=True))
        p_loc = jnp.exp(s_loc - m)
        p_ctx = jnp.exp(s_ctx - m)
        denom = jnp.sum(p_loc, axis=-1, keepdims=True) + jnp.sum(p_ctx, axis=-1, keepdims=True)
        p_loc = p_loc.astype(BF16)
        o_loc = [_dot(p_loc[j * width:(j + 1) * width], vb[krows[j], :]) for j in range(rg)]
        o_loc = jnp.concatenate(o_loc, axis=0) if rg > 1 else o_loc[0]
        o = o_loc + _dot(p_ctx.astype(BF16), vc_ref[0, 0, 0].astype(BF16))
        o_ref[qrows, :] = (o / denom).astype(BF16)
        return c

    lax.fori_loop(0, n_rows // rg, row_group, 0)


def _natten_bias(rpb, width, kr):
    kc = NAT_KC
    cols = np.arange(width)
    c0 = np.clip(cols - kc // 2, 0, width - kc)
    j = np.arange(width)
    in_win = (j[None, :] >= c0[:, None]) & (j[None, :] < c0[:, None] + kc)
    ci = np.clip(j[None, :] - cols[:, None] + kc - 1, 0, 2 * kc - 2)
    off = np.arange(kr)
    ri = np.arange(kr)[None, :] - off[:, None] + NAT_KR - 1
    pick_r = np.zeros((kr * kr, 2 * NAT_KR - 1), np.float32)
    pick_r[np.arange(kr * kr), ri.reshape(-1)] = 1.0
    pick_c = np.zeros((2 * kc - 1, width * width), np.float32)
    pick_c[ci.reshape(-1), np.arange(width * width)] = 1.0
    tab = jnp.einsum('ar,hrs,sb->hab', pick_r, rpb.astype(F32), pick_c, precision=lax.Precision.HIGHEST)
    tab = tab.reshape(rpb.shape[0], kr, kr, width, width).transpose(0, 1, 3, 2, 4)
    tab = jnp.where(in_win[None, None, :, None, :], tab, MASKED)
    return tab.reshape(rpb.shape[0], kr, width, kr * width)


def _natten(proj, seq_len, q_gain, k_gain, rpb, kc, vc, layer):
    n = proj.shape[0]
    nseq = n // seq_len
    hd = HEAD_DIM
    n_heads = rpb.shape[0]
    width = GRID_W
    kr = min(NAT_KR, seq_len // width)
    bias = _natten_bias(rpb, width, kr)
    cache_len = kc.shape[3]
    chunk = min(512, seq_len)
    col = lambda c0: pl.BlockSpec((seq_len, hd), lambda b, h: (b, c0 // hd + h))
    return pl.pallas_call(
        functools.partial(_natten_kernel, seq=seq_len, width=width, kr=kr, chunk=chunk),
        out_shape=jax.ShapeDtypeStruct((n, n_heads * hd), BF16),
        grid=(nseq, n_heads),
        in_specs=[col(COL_NQ), col(COL_NK), col(COL_NV),
                  pl.BlockSpec((1, hd), lambda b, h: (0, 0)),
                  pl.BlockSpec((1, hd), lambda b, h: (0, 0)),
                  pl.BlockSpec((1, 1, 1, cache_len, hd), lambda b, h: (b, layer, h, 0, 0)),
                  pl.BlockSpec((1, 1, 1, cache_len, hd), lambda b, h: (b, layer, h, 0, 0)),
                  pl.BlockSpec((1, kr, width, kr * width), lambda b, h: (h, 0, 0, 0))],
        out_specs=pl.BlockSpec((seq_len, hd), lambda b, h: (b, h)),
        scratch_shapes=[pltpu.VMEM((seq_len, hd), BF16), pltpu.VMEM((seq_len, hd), BF16)],
        compiler_params=_cparams(2),
        name="natten_latent",
    )(proj, proj, proj, q_gain.reshape(1, hd), k_gain.reshape(1, hd), kc, vc, bias)


def _retention_kernel(q_ref, k_ref, v_ref, rg_ref, lg_ref, gn_ref, sf0_ref, sb0_ref,
                      y_ref, sf_ref, sb_ref, o_scr, kv_scr, sp_scr, *, seq, unroll):
    c = RET_CHUNK
    nc = seq // c
    hd = HEAD_DIM
    u = unroll
    lgf = lg_ref[0, 0:1, :]
    lgb = lg_ref[0, 1:2, :]
    ii = lax.broadcasted_iota(jnp.int32, (c, c), 0)
    jj = lax.broadcasted_iota(jnp.int32, (c, c), 1)
    rel = (ii - jj).astype(F32)
    pos = lax.broadcasted_iota(jnp.int32, (c, hd), 0).astype(F32)
    decay = (jnp.where(rel >= 0, jnp.exp(lgf * jnp.maximum(rel, 0.0)), 0.0)
             + jnp.where(rel <= 0, jnp.exp(lgb * jnp.maximum(-rel, 0.0)), 0.0))
    qd = jnp.concatenate([jnp.exp(lgf * (pos + 1.0)), jnp.exp(lgb * (c - pos))], axis=1)
    kd = jnp.concatenate([jnp.exp(lgf * (c - 1.0 - pos)), jnp.exp(lgb * pos)], axis=1)
    cd_f = jnp.exp(lgf * c)
    cd_b = jnp.exp(lgb * c)
    kscale = HEAD_DIM ** -0.5

    def within_chunks(gi, carry):
        rows = [_rows(gi * u + j, c) for j in range(u)]
        k = [k_ref[r, :] * kscale for r in rows]
        v = [v_ref[r, :].astype(BF16) for r in rows]
        qk = [_dot_nt(q_ref[rows[j], :].astype(BF16), k[j].astype(BF16)) for j in range(u)]
        att = [(s * decay).astype(BF16) for s in qk]
        for j in range(u):
            o_scr[rows[j], :] = _dot(att[j], v[j])
        for j in range(u):
            kk = jnp.concatenate([k[j], k[j]], axis=1) * kd
            kv_scr[gi * u + j] = _dot(kk.T.astype(BF16), v[j])
        return carry

    lax.fori_loop(0, nc // u, within_chunks, 0)

    def recurrence(t, carry):
        sf, sb = carry
        tb = nc - 1 - t
        sp_scr[t, 0:hd, :] = sf.astype(BF16)
        sp_scr[tb, hd:2 * hd, :] = sb.astype(BF16)
        return sf * cd_f + kv_scr[t, 0:hd, :], sb * cd_b + kv_scr[tb, hd:2 * hd, :]

    sf, sb = lax.fori_loop(0, nc, recurrence, (sf0_ref[0, 0, 0], sb0_ref[0, 0, 0]))
    sf_ref[0, 0] = sf
    sb_ref[0, 0] = sb

    def across_chunks(gi, carry):
        rows = [_rows(gi * u + j, c) for j in range(u)]
        o = []
        for j in range(u):
            q = q_ref[rows[j], :]
            qq = (jnp.concatenate([q, q], axis=1) * qd).astype(BF16)
            o.append(o_scr[rows[j], :] + _dot(qq, sp_scr[gi * u + j]))
        for j in range(u):
            on = o[j] * lax.rsqrt(jnp.mean(o[j] * o[j], axis=-1, keepdims=True) + EPS)
            g = rg_ref[rows[j], :]
            y_ref[rows[j], :] = ((g * _sigmoid(g)) * (on * gn_ref[...])).astype(BF16)
        return carry

    lax.fori_loop(0, nc // u, across_chunks, 0)


def _retention(proj, seq_len, log_gamma, ret_gn, s_fwd, s_bwd, layer):
    n = proj.shape[0]
    nseq = n // seq_len
    hd = HEAD_DIM
    n_heads = log_gamma.shape[0]
    nc = seq_len // RET_CHUNK
    shared = s_fwd.shape[0] == 1
    col = lambda c0: pl.BlockSpec((seq_len, hd), lambda b, h: (b, c0 // hd + h))
    state = pl.BlockSpec((1, 1, 1, hd, hd), lambda b, h: (0 if shared else b, layer, 0 if shared else h, 0, 0))
    return pl.pallas_call(
        functools.partial(_retention_kernel, seq=seq_len, unroll=next(u for u in (RET_UNROLL, 2, 1) if nc % u == 0)),
        out_shape=[jax.ShapeDtypeStruct((n, n_heads * hd), BF16),
                   jax.ShapeDtypeStruct((nseq, n_heads, hd, hd), F32),
                   jax.ShapeDtypeStruct((nseq, n_heads, hd, hd), F32)],
        grid=(nseq, n_heads),
        in_specs=[col(COL_RQ), col(COL_RK), col(COL_RV), col(COL_RG),
                  pl.BlockSpec((1, 2, hd), lambda b, h: (h, 0, 0)),
                  pl.BlockSpec((1, hd), lambda b, h: (0, h)),
                  state, state],
        out_specs=[pl.BlockSpec((seq_len, hd), lambda b, h: (b, h)),
                   pl.BlockSpec((1, 1, hd, hd), lambda b, h: (b, h, 0, 0)),
                   pl.BlockSpec((1, 1, hd, hd), lambda b, h: (b, h, 0, 0))],
        scratch_shapes=[pltpu.VMEM((seq_len, hd), F32), pltpu.VMEM((nc, 2 * hd, hd), F32),
                        pltpu.VMEM((nc, 2 * hd, hd), BF16)],
        compiler_params=_cparams(2),
        name="retention",
    )(proj, proj, proj, proj, log_gamma, ret_gn.reshape(1, n_heads * hd), s_fwd, s_bwd)


def _merge_kernel(y0_ref, y1_ref, y2_ref, y3_ref, g0_ref, g1_ref, g2_ref, g3_ref, wb_ref, wo_ref, o_ref):
    k = pl.program_id(1)
    m = g0_ref[...] * _dot(y0_ref[...], wb_ref[0])
    for i, (y_ref, g_ref) in enumerate(((y1_ref, g1_ref), (y2_ref, g2_ref), (y3_ref, g3_ref)), start=1):
        m = m + g_ref[...] * _dot(y_ref[...], wb_ref[i])
    part = _dot(m.astype(BF16), wo_ref[...])

    @pl.when(k == 0)
    def _():
        o_ref[...] = part

    @pl.when(k > 0)
    def _():
        o_ref[...] += part


def _merge(branches, gates, w_branch, w_out):
    n = gates.shape[0]
    _, bw, d = w_branch.shape
    tm, tk = min(MERGE_TM, n), min(MERGE_TK, d)
    nk = d // tk
    gate = lambda i: pl.BlockSpec((tm, tk), lambda t, k: (t, i * nk + k))
    return pl.pallas_call(
        _merge_kernel,
        out_shape=jax.ShapeDtypeStruct((n, d), F32),
        grid=(n // tm, nk),
        in_specs=[pl.BlockSpec((tm, bw), lambda t, k: (t, 0))] * 4
                 + [gate(0), gate(1), gate(2), gate(3),
                    pl.BlockSpec((N_BRANCH, bw, tk), lambda t, k: (0, 0, k)),
                    pl.BlockSpec((tk, d), lambda t, k: (k, 0))],
        out_specs=pl.BlockSpec((tm, d), lambda t, k: (t, 0)),
        compiler_params=_cparams(2),
        name="merge_out",
    )(*branches, gates, gates, gates, gates, w_branch, w_out)


def _router_kernel(x_ref, a_ref, mod_ref, g_ref, wt_ref, b_ref, x1_ref, h_ref, eid_ref, wts_ref):
    x1 = x_ref[...] + mod_ref[0, 2:3, :] * a_ref[...]
    x1_ref[...] = x1
    y = _rms(x1, g_ref[...])
    h = y * (1.0 + mod_ref[0, 4:5, :]) + mod_ref[0, 3:4, :]
    h_ref[...] = h
    h_hi = h.astype(BF16)
    h_lo = (h - h_hi.astype(F32)).astype(BF16)
    w = wt_ref[...]
    w_hi = w.astype(BF16)
    w_lo = (w - w_hi.astype(F32)).astype(BF16)
    logits = _dot_nt(w_hi, h_hi) + (_dot_nt(w_lo, h_hi) + _dot_nt(w_hi, h_lo))
    score = _sigmoid(logits)
    sel = score + b_ref[...]
    epg = N_EXPERTS // N_GROUPS
    s = [sel[e:e + 1, :] for e in range(N_EXPERTS)]
    sc = [score[e:e + 1, :] for e in range(N_EXPERTS)]

    def group_score(vals):
        best = None
        for a in range(len(vals)):
            for b in range(a + 1, len(vals)):
                pair = vals[a] + vals[b]
                best = pair if best is None else jnp.maximum(best, pair)
        return best

    gs = [group_score(s[g * epg:(g + 1) * epg]) for g in range(N_GROUPS)]
    g_best = jnp.zeros(gs[0].shape, jnp.int32)
    best = gs[0]
    for g in range(1, N_GROUPS):
        better = gs[g] > best
        g_best = jnp.where(better, g, g_best)
        best = jnp.where(better, gs[g], best)
    in_sel, in_score = [], []
    for k in range(epg):
        v, w_ = s[k], sc[k]
        for g in range(1, N_GROUPS):
            v = jnp.where(g_best == g, s[g * epg + k], v)
            w_ = jnp.where(g_best == g, sc[g * epg + k], w_)
        in_sel.append(v)
        in_score.append(w_)
    i1 = jnp.zeros(g_best.shape, jnp.int32)
    v1, w1 = in_sel[0], in_score[0]
    for k in range(1, epg):
        better = in_sel[k] > v1
        i1 = jnp.where(better, k, i1)
        v1 = jnp.where(better, in_sel[k], v1)
        w1 = jnp.where(better, in_score[k], w1)
    i2 = jnp.zeros(g_best.shape, jnp.int32)
    v2 = jnp.full(v1.shape, -jnp.inf, F32)
    w2 = jnp.zeros(v1.shape, F32)
    for k in range(epg):
        better = (i1 != k) & (in_sel[k] > v2)
        i2 = jnp.where(better, k, i2)
        v2 = jnp.where(better, in_sel[k], v2)
        w2 = jnp.where(better, in_score[k], w2)
    total = w1 + w2
    eid_ref[...] = jnp.concatenate([g_best * epg + i1, g_best * epg + i2], axis=0)
    wts_ref[...] = jnp.concatenate([w1 / total, w2 / total], axis=0)


def _router(x, mixed, mods, mod_row0, tiles_per_seq, gain, w_router_t, b_router):
    n, d = x.shape
    tm = min(ROUTER_TM, n)
    tps = max(tiles_per_seq // tm, 1) if tiles_per_seq else n // tm
    e = w_router_t.shape[0]
    tile = pl.BlockSpec((tm, d), lambda i: (i, 0))
    return pl.pallas_call(
        _router_kernel,
        out_shape=[jax.ShapeDtypeStruct((n, d), F32),
                   jax.ShapeDtypeStruct((n, d), F32),
                   jax.ShapeDtypeStruct((TOP_K, n), jnp.int32),
                   jax.ShapeDtypeStruct((TOP_K, n), F32)],
        grid=(n // tm,),
        in_specs=[tile, tile,
                  pl.BlockSpec((1, 6, d), lambda i: (mod_row0 + i // tps, 0, 0)),
                  pl.BlockSpec((1, d), lambda i: (0, 0)),
                  pl.BlockSpec((e, d), lambda i: (0, 0)),
                  pl.BlockSpec((e, 1), lambda i: (0, 0))],
        out_specs=[tile, tile,
                   pl.BlockSpec((TOP_K, tm), lambda i: (0, i)),
                   pl.BlockSpec((TOP_K, tm), lambda i: (0, i))],
        compiler_params=_cparams(1),
        name="router",
    )(x, mixed, mods, gain.reshape(1, d), w_router_t, b_router.reshape(e, 1))


def _row_copy(src_ref, src_row, dst_ref, dst_row, sem):
    return pltpu.make_async_copy(src_ref.at[pl.ds(src_row, 1)], dst_ref.at[pl.ds(dst_row, 1)], sem)


def _dispatch_kernel(slot_ref, h_ref, buf_in_ref, buf_ref, sem, *, tt):
    del buf_in_ref

    def copies(j):
        return [_row_copy(h_ref, j, buf_ref, slot_ref[0, 0, k * tt + j], sem) for k in range(TOP_K)]

    def start(j, c):
        for k, cp in enumerate(copies(j)):
            cp.start(priority=k % 2)
        return c

    def wait(j, c):
        for cp in copies(j):
            cp.wait()
        return c

    lax.fori_loop(0, tt, start, 0, unroll=DMA_UNROLL)
    lax.fori_loop(0, tt, wait, 0, unroll=DMA_UNROLL)


def _dispatch(h, slots, buf):
    n, d = h.shape
    tt = min(MOE_TT, n)
    return pl.pallas_call(
        functools.partial(_dispatch_kernel, tt=tt),
        out_shape=jax.ShapeDtypeStruct(buf.shape, buf.dtype),
        grid=(n // tt,),
        in_specs=[pl.BlockSpec((1, 1, TOP_K * tt), lambda i: (i, 0, 0), memory_space=pltpu.SMEM),
                  pl.BlockSpec((tt, d), lambda i: (i, 0)),
                  pl.BlockSpec(memory_space=pl.ANY)],
        out_specs=pl.BlockSpec(memory_space=pl.ANY),
        scratch_shapes=[pltpu.SemaphoreType.DMA],
        input_output_aliases={2: 0},
        compiler_params=_cparams(1),
        name="moe_dispatch",
    )(slots, h, buf)


def _ffn_kernel(be_ref, nu_ref, x_ref, wg_ref, wu_ref, wd_ref, o_ref):
    del be_ref
    live = pl.program_id(0) < nu_ref[0]

    @pl.when(live)
    def _():
        x = x_ref[...].astype(BF16)
        a = _dot(x, wg_ref[0])
        b = _dot(x, wu_ref[0])
        o_ref[...] = _dot(((a * _sigmoid(a)) * b).astype(BF16), wd_ref[0])

    @pl.when(jnp.logical_not(live))
    def _():
        o_ref[...] = jnp.zeros(o_ref.shape, F32)


def _expert_ffn(buf, block_expert, n_used, w_gate, w_up, w_down):
    rows, d = buf.shape
    de = w_gate.shape[2]
    rb = MOE_ROWS
    grid_spec = pltpu.PrefetchScalarGridSpec(
        num_scalar_prefetch=2,
        grid=(rows // rb,),
        in_specs=[pl.BlockSpec((rb, d), lambda i, be, nu: (i, 0)),
                  pl.BlockSpec((1, d, de), lambda i, be, nu: (be[i], 0, 0)),
                  pl.BlockSpec((1, d, de), lambda i, be, nu: (be[i], 0, 0)),
                  pl.BlockSpec((1, de, d), lambda i, be, nu: (be[i], 0, 0))],
        out_specs=pl.BlockSpec((rb, d), lambda i, be, nu: (i, 0)))
    return pl.pallas_call(
        _ffn_kernel,
        out_shape=jax.ShapeDtypeStruct((rows, d), F32),
        grid_spec=grid_spec,
        compiler_params=_cparams(1),
        name="expert_ffn",
    )(block_expert, n_used, buf, w_gate, w_up, w_down)


def _combine_kernel(slot_ref, yb_ref, x_ref, w_ref, mod_ref, o_ref, g0, g1, sem, *, tt):
    bufs = (g0, g1)

    def copies(j):
        return [_row_copy(yb_ref, slot_ref[0, 0, k * tt + j], bufs[k], j, sem) for k in range(TOP_K)]

    def start(j, c):
        for k, cp in enumerate(copies(j)):
            cp.start(priority=k % 2)
        return c

    def wait(j, c):
        for cp in copies(j):
            cp.wait()
        return c

    lax.fori_loop(0, tt, start, 0, unroll=DMA_UNROLL)
    lax.fori_loop(0, tt, wait, 0, unroll=DMA_UNROLL)
    w = w_ref[...]
    y = w[:, 0:1] * g0[...] + w[:, 1:2] * g1[...]
    o_ref[...] = x_ref[...] + mod_ref[0, 5:6, :] * y


def _combine(x, yb, slots, wts, mods, mod_row0, tiles_per_seq):
    n, d = x.shape
    tt = min(MOE_TT, n)
    tps = max(tiles_per_seq // tt, 1) if tiles_per_seq else n // tt
    return pl.pallas_call(
        functools.partial(_combine_kernel, tt=tt),
        out_shape=jax.ShapeDtypeStruct((n, d), F32),
        grid=(n // tt,),
        in_specs=[pl.BlockSpec((1, 1, TOP_K * tt), lambda i: (i, 0, 0), memory_space=pltpu.SMEM),
                  pl.BlockSpec(memory_space=pl.ANY),
                  pl.BlockSpec((tt, d), lambda i: (i, 0)),
                  pl.BlockSpec((tt, TOP_K), lambda i: (i, 0)),
                  pl.BlockSpec((1, 6, d), lambda i: (mod_row0 + i // tps, 0, 0))],
        out_specs=pl.BlockSpec((tt, d), lambda i: (i, 0)),
        scratch_shapes=[pltpu.VMEM((tt, d), F32), pltpu.VMEM((tt, d), F32), pltpu.SemaphoreType.DMA],
        compiler_params=_cparams(1),
        name="moe_combine",
    )(slots, yb, x, wts, mods)


def _slot_blocks(slots, tt):
    k, n = slots.shape
    return slots.reshape(k, n // tt, tt).transpose(1, 0, 2).reshape(n // tt, 1, k * tt)


def _moe(xs, mixed, mods, mod_rows, seq_lens, gain, w_router, b_router, w_gate, w_up, w_down, spare=None):
    d = xs[0].shape[1]
    w_router_t = w_router.T
    routed = [_router(x, a, mods, r0, sl, gain, w_router_t, b_router)
              for x, a, r0, sl in zip(xs, mixed, mod_rows, seq_lens)]
    flat_e = jnp.concatenate([r[2].reshape(-1) for r in routed])
    n_assign = flat_e.shape[0]
    onehot = (flat_e[:, None] == jnp.arange(N_EXPERTS, dtype=jnp.int32)[None, :]).astype(jnp.int32)
    csum = jnp.cumsum(onehot, axis=0)
    counts = csum[-1]
    padded = (counts + MOE_ROWS - 1) // MOE_ROWS * MOE_ROWS
    pend = jnp.cumsum(padded)
    slot = jnp.sum(onehot * (csum - 1 + (pend - padded)[None, :]), axis=1)
    n_blocks = -(-n_assign // MOE_ROWS) + N_EXPERTS
    if spare is not None and spare.shape == (n_blocks * MOE_ROWS, d):
        buf = spare
    else:
        buf = jnp.zeros((n_blocks * MOE_ROWS, d), F32)
    block_start = jnp.arange(n_blocks, dtype=jnp.int32) * MOE_ROWS
    block_expert = jnp.minimum(jnp.sum((pend[None, :] <= block_start[:, None]).astype(jnp.int32), axis=1),
                               N_EXPERTS - 1)
    n_used = (pend[-1:] // MOE_ROWS).astype(jnp.int32)
    slot_blocks, off = [], 0
    for r in routed:
        n = r[0].shape[0]
        sb = _slot_blocks(slot[off:off + TOP_K * n].reshape(TOP_K, n), min(MOE_TT, n))
        off += TOP_K * n
        slot_blocks.append(sb)
        buf = _dispatch(r[1], sb, buf)
    yb = _expert_ffn(buf, block_expert, n_used, w_gate, w_up, w_down)
    outs = [_combine(r[0], yb, sb, r[3].T, mods, r0, sl)
            for r, sb, r0, sl in zip(routed, slot_blocks, mod_rows, seq_lens)]
    return outs, buf


def _rope_tables(n_tok):
    t = jnp.arange(n_tok)
    row = (t // GRID_W).astype(F32)
    col = (t % GRID_W).astype(F32)
    quarter = HEAD_DIM // 4
    inv = ROPE_THETA ** (-jnp.arange(quarter, dtype=F32) / quarter)
    ar = row[:, None] * inv
    ac = col[:, None] * inv
    ang = jnp.concatenate([ar, ar, ac, ac], axis=-1)
    cos, sin = jnp.cos(ang), jnp.sin(ang)
    first = (jnp.arange(HEAD_DIM) % (2 * quarter)) < quarter
    return cos, jnp.where(first, -sin, 0.0), jnp.where(first, 0.0, sin)


def kernel(x_prompt, x_sample, cache_nat_k, cache_nat_v, cache_gqa_k, cache_gqa_v, state_ret_fwd,
           state_ret_bwd, c, c_ctx, w_mod, b_mod, norm1, norm2, w_in, conv_w, nat_qn, nat_kn, nat_rpb,
           gqa_qn, gqa_kn, ret_decay_fwd, ret_decay_bwd, ret_gn, w_branch, w_out, w_router, b_router,
           w_exp_gate, w_exp_up, w_exp_down):
    batch, seq, d = x_prompt.shape
    dec_batch, dec_seq, _ = x_sample.shape
    depth = w_mod.shape[0]
    n_heads = nat_rpb.shape[1]
    n_kv = cache_gqa_k.shape[2]
    hd = HEAD_DIM

    xp = x_prompt.reshape(batch * seq, d)
    xs = x_sample.reshape(dec_batch * dec_seq, d)
    cvecs = jnp.zeros((SUBLANES, d), F32).at[0].set(c_ctx).at[1:1 + dec_batch].set(c)
    mods_all = _modulation(cvecs, w_mod, b_mod).reshape(depth, SUBLANES, 6, d)
    rope_tabs = _rope_tables(dec_seq)
    zero_state = jnp.zeros((1, depth, 1, hd, hd), F32)

    caches = [[] for _ in range(6)]
    moe_buf = None
    for l in range(depth):
        mods = mods_all[l]
        gate_w = w_in.shape[2] - PROJ_W
        wb = w_branch[l].astype(BF16)
        wo = w_out[l].astype(BF16)
        lg = jnp.stack([jax.nn.log_sigmoid(ret_decay_fwd[l].astype(F32)),
                        jax.nn.log_sigmoid(ret_decay_bwd[l].astype(F32))], axis=1)
        lg = jnp.broadcast_to(lg[:, :, None], (lg.shape[0], 2, hd))

        pc = _norm_project(xp, mods, 0, 0, norm1[l], w_in, l, 0, PROJ_W, gate=False, name="in_proj_ctx")
        gc = _norm_project(xp, mods, 0, 0, norm1[l], w_in, l, PROJ_W, gate_w, gate=True, name="gate_proj_ctx")
        conv_c = _short_conv(pc, conv_w[l], seq)
        nat_c, nk, nv = _attention(pc, seq, n_heads, 1, COL_NQ, COL_NK, COL_NV, nat_qn[l], nat_kn[l],
                                   emit_kv=True, name="nat_ctx")
        gqa_c, gk, gv = _attention(pc, seq, n_kv, GQA_GROUP, COL_GQ, COL_GK, COL_GV, gqa_qn[l], gqa_kn[l],
                                   emit_kv=True, name="gqa_ctx")
        ret_c, s_f, s_b = _retention(pc, seq, lg, ret_gn[l], zero_state, zero_state, l)
        for lst, val in zip(caches, (nk, nv, gk, gv, s_f, s_b)):
            lst.append(val)
        mix_c = _merge((conv_c, nat_c, gqa_c, ret_c), gc, wb, wo)

        pl_ = _norm_project(xs, mods, 1, dec_seq, norm1[l], w_in, l, 0, PROJ_W, gate=False, name="in_proj_lat")
        gl = _norm_project(xs, mods, 1, dec_seq, norm1[l], w_in, l, PROJ_W, gate_w, gate=True,
                           name="gate_proj_lat")
        conv_l = _short_conv(pl_, conv_w[l], dec_seq)
        nat_l = _natten(pl_, dec_seq, nat_qn[l], nat_kn[l], nat_rpb[l], cache_nat_k, cache_nat_v, l)
        gqa_l = _attention(pl_, dec_seq, n_kv, GQA_GROUP, COL_GQ, COL_GK, COL_GV, gqa_qn[l], gqa_kn[l],
                           rope_tabs=rope_tabs, cache=(cache_gqa_k, cache_gqa_v, l), name="gqa_lat")
        ret_l, _, _ = _retention(pl_, dec_seq, lg, ret_gn[l], state_ret_fwd, state_ret_bwd, l)
        mix_l = _merge((conv_l, nat_l, gqa_l, ret_l), gl, wb, wo)

        (xp, xs), moe_buf = _moe([xp, xs], [mix_c, mix_l], mods, [0, 1], [0, dec_seq], norm2[l], w_router,
                                 b_router, w_exp_gate[l].astype(BF16), w_exp_up[l].astype(BF16),
                                 w_exp_down[l].astype(BF16), spare=moe_buf)

    outs = [jnp.stack(v, axis=1) for v in caches]
    return (xp.reshape(batch, seq, d), xs.reshape(dec_batch, dec_seq, d), *outs)
```

```python
import functools

import numpy as np
import jax
import jax.numpy as jnp
from jax import lax
from jax.experimental import pallas as pl
from jax.experimental.pallas import tpu as pltpu

F32 = jnp.float32
BF16 = jnp.bfloat16

GRID_W = 64
HEAD_DIM = 128
BRANCH_W = 512
N_BRANCH = 4
CONV_K = 3
NAT_KR = 8
NAT_KC = 16
GQA_GROUP = 2
RET_CHUNK = 128
ROPE_THETA = 10000.0
N_EXPERTS = 16
N_GROUPS = 4
TOP_K = 2
EPS = 1e-6
MASKED = -1e30
LOG2_E = 1.4426950408889634

COL_U, COL_BG, COL_CG = 0, 512, 1024
COL_NQ, COL_NK, COL_NV = 1536, 2048, 2560
COL_GQ, COL_GK, COL_GV = 3072, 3584, 3840
COL_RQ, COL_RK, COL_RV, COL_RG = 4096, 4608, 5120, 5632
PROJ_W = 6144

V7X_VMEM_LIMIT_BYTES = 56 * 1024 * 1024
SUBLANES = 8
LANES = 128

PROJ_TM, PROJ_TN = 1024, 1024
NORM_ROWS = 256
MERGE_TM, MERGE_TK = 1024, 512
ROUTER_TM = 512
MOE_ROWS = 256
MOE_TT = 256
CONV_TT = 1024
ATT_TQ, ATT_TK = 512, 1024
MOD_TN = 1024
NAT_ROWS = 16
RET_UNROLL = 4
DMA_UNROLL = 8


def _cparams(n_axes):
    return pltpu.CompilerParams(dimension_semantics=("arbitrary",) * n_axes,
                                vmem_limit_bytes=V7X_VMEM_LIMIT_BYTES)


def _sigmoid(x):
    return 1.0 / (1.0 + jnp.exp(-x))


def _dot(a, b):
    return jnp.dot(a, b, preferred_element_type=F32)


def _dot_nt(a, b):
    return lax.dot_general(a, b, (((1,), (1,)), ((), ())), preferred_element_type=F32)


def _rms(x, gain):
    return x * lax.rsqrt(jnp.mean(x * x, axis=-1, keepdims=True) + EPS) * gain


def _rows(i, n):
    return pl.ds(pl.multiple_of(i * n, n), n)


def _mod_kernel(c_ref, w_ref, b_ref, o_ref):
    c = c_ref[...]
    a = (c * _sigmoid(c)).astype(BF16)
    o_ref[0] = _dot(a, w_ref[0].astype(BF16)) + b_ref[0]


def _modulation(cvecs, w_mod, b_mod):
    depth, d, n = w_mod.shape
    tn = min(MOD_TN, n)
    return pl.pallas_call(
        _mod_kernel,
        out_shape=jax.ShapeDtypeStruct((depth, cvecs.shape[0], n), F32),
        grid=(depth, n // tn),
        in_specs=[pl.BlockSpec(cvecs.shape, lambda l, j: (0, 0)),
                  pl.BlockSpec((1, d, tn), lambda l, j: (l, 0, j)),
                  pl.BlockSpec((1, 1, tn), lambda l, j: (l, 0, j))],
        out_specs=pl.BlockSpec((1, cvecs.shape[0], tn), lambda l, j: (l, 0, j)),
        compiler_params=_cparams(2),
        name="modulation",
    )(cvecs, w_mod, b_mod.reshape(depth, 1, n))


def _normproj_kernel(x_ref, mod_ref, g_ref, w_ref, o_ref, h_scr, *, gate):
    @pl.when(pl.program_id(1) == 0)
    def _():
        gain = g_ref[...] * (1.0 + mod_ref[0, 1:2, :])
        shift = mod_ref[0, 0:1, :]

        def chunk(i, c):
            rows = _rows(i, NORM_ROWS)
            x = x_ref[rows, :]
            y = x * lax.rsqrt(jnp.mean(x * x, axis=-1, keepdims=True) + EPS)
            h_scr[rows, :] = (y * gain + shift).astype(BF16)
            return c

        lax.fori_loop(0, x_ref.shape[0] // NORM_ROWS, chunk, 0)

    acc = _dot(h_scr[...], w_ref[0])
    o_ref[...] = _sigmoid(acc) if gate else acc


def _norm_project(x, mods, mod_row0, tiles_per_seq, gain, w_all, layer, col0, nout, *, gate, name):
    n, d = x.shape
    tm, tn = min(PROJ_TM, n), min(PROJ_TN, nout)
    assert col0 % tn == 0 and nout % tn == 0
    tps = max(tiles_per_seq // tm, 1) if tiles_per_seq else n // tm
    return pl.pallas_call(
        functools.partial(_normproj_kernel, gate=gate),
        out_shape=jax.ShapeDtypeStruct((n, nout), F32),
        grid=(n // tm, nout // tn),
        in_specs=[pl.BlockSpec((tm, d), lambda i, j: (i, 0)),
                  pl.BlockSpec((1, 6, d), lambda i, j: (mod_row0 + i // tps, 0, 0)),
                  pl.BlockSpec((1, d), lambda i, j: (0, 0)),
                  pl.BlockSpec((1, d, tn), lambda i, j: (layer, 0, col0 // tn + j))],
        out_specs=pl.BlockSpec((tm, tn), lambda i, j: (i, j)),
        scratch_shapes=[pltpu.VMEM((tm, d), BF16)],
        compiler_params=_cparams(2),
        name=name,
    )(x, mods, gain.reshape(1, d), w_all)


def _conv_kernel(u_ref, bg_ref, cg_ref, up_ref, cp_ref, un_ref, cn_ref, w_ref, o_ref, *, tt, nt):
    t = pl.program_id(1)
    z = cg_ref[...] * u_ref[...]
    last = SUBLANES - 1
    z_before = jnp.where(t > 0, cp_ref[last:last + 1, :] * up_ref[last:last + 1, :], 0.0)
    z_after = jnp.where(t < nt - 1, cn_ref[0:1, :] * un_ref[0:1, :], 0.0)
    ri = lax.broadcasted_iota(jnp.int32, z.shape, 0)
    z_prev = jnp.where(ri == 0, z_before, pltpu.roll(z, 1, 0))
    z_next = jnp.where(ri == tt - 1, z_after, pltpu.roll(z, tt - 1, 0))
    w = w_ref[...]
    o_ref[...] = (bg_ref[...] * (w[0:1, :] * z_prev + w[1:2, :] * z + w[2:3, :] * z_next)).astype(BF16)


def _short_conv(proj, conv_w, seq_len):
    n = proj.shape[0]
    cw = conv_w.shape[1]
    nseq = n // seq_len
    tt = min(CONV_TT, seq_len)
    nt = seq_len // tt
    cu, cb, cc = COL_U // cw, COL_BG // cw, COL_CG // cw
    nblk8 = n // SUBLANES

    def main(col):
        return pl.BlockSpec((tt, cw), lambda b, t: (b * nt + t, col))

    def before(col):
        return pl.BlockSpec((SUBLANES, cw),
                            lambda b, t: (jnp.maximum((b * nt + t) * (tt // SUBLANES) - 1, 0), col))

    def after(col):
        return pl.BlockSpec((SUBLANES, cw),
                            lambda b, t: (jnp.minimum((b * nt + t + 1) * (tt // SUBLANES), nblk8 - 1), col))

    return pl.pallas_call(
        functools.partial(_conv_kernel, tt=tt, nt=nt),
        out_shape=jax.ShapeDtypeStruct((n, cw), BF16),
        grid=(nseq, nt),
        in_specs=[main(cu), main(cb), main(cc), before(cu), before(cc), after(cu), after(cc),
                  pl.BlockSpec(conv_w.shape, lambda b, t: (0, 0))],
        out_specs=pl.BlockSpec((tt, cw), lambda b, t: (b * nt + t, 0)),
        compiler_params=_cparams(2),
        name="short_conv",
    )(proj, proj, proj, proj, proj, proj, proj, conv_w)


def _attn_kernel(*refs, group, seq, tq, tk, rope, cache_len, emit_kv):
    it = iter(refs)
    q_ref, k_ref, v_ref, qn_ref, kn_ref = (next(it) for _ in range(5))
    if rope:
        cos_ref, sin_lo_ref, sin_hi_ref = (next(it) for _ in range(3))
    if cache_len:
        kc_ref, vc_ref = next(it), next(it)
    o_ref = next(it)
    if emit_kv:
        ko_ref, vo_ref = next(it), next(it)
    kb, vb, m_scr, l_scr, acc, q_scr = (next(it) for _ in range(6))
    scale = HEAD_DIM ** -0.5 * LOG2_E
    hd = HEAD_DIM

    def prep(x, gain_ref, rows):
        y = _rms(x, gain_ref[...])
        if rope:
            y = (y * cos_ref[rows, :] + pltpu.roll(y, hd - hd // 4, 1) * sin_lo_ref[rows, :]
                 + pltpu.roll(y, hd // 4, 1) * sin_hi_ref[rows, :])
        return y

    def key_tile(kt, c):
        rows = _rows(kt, tk)
        kn = prep(k_ref[rows, :], kn_ref, rows)
        v = v_ref[rows, :]
        kb[rows, :] = kn.astype(BF16)
        vb[rows, :] = v.astype(BF16)
        if emit_kv:
            ko_ref[0, 0, rows, :] = kn
            vo_ref[0, 0, rows, :] = v
        return c

    lax.fori_loop(0, seq // tk, key_tile, 0)

    def softmax_step(s, vblk):
        m_prev = m_scr[...]
        m_new = jnp.maximum(m_prev, jnp.max(s, axis=-1, keepdims=True))
        alpha = jnp.exp2(m_prev - m_new)
        p = jnp.exp2(s - jnp.concatenate([m_new] * (s.shape[1] // hd), axis=1))
        l_scr[...] = alpha * l_scr[...] + jnp.sum(p, axis=-1, keepdims=True)
        acc[...] = alpha * acc[...] + _dot(p.astype(BF16), vblk)
        m_scr[...] = m_new

    def query_tile(qt, c):
        rows = _rows(qt, tq)
        for g in range(group):
            q_scr[g * tq:(g + 1) * tq, :] = (
                prep(q_ref[rows, g * hd:(g + 1) * hd], qn_ref, rows) * scale).astype(BF16)
        m_scr[...] = jnp.full(m_scr.shape, MASKED, F32)
        l_scr[...] = jnp.zeros(l_scr.shape, F32)
        acc[...] = jnp.zeros(acc.shape, F32)
        n_kv = seq // tk

        def scores(kt):
            return _dot_nt(q_scr[...], kb[_rows(kt, tk), :])

        def kv_tile(kt, c2):
            softmax_step(scores(kt), vb[_rows(kt, tk), :])
            return c2

        lax.fori_loop(0, n_kv, kv_tile, 0)
        if cache_len:
            softmax_step(_dot_nt(q_scr[...], kc_ref[0, 0, 0].astype(BF16)), vc_ref[0, 0, 0].astype(BF16))
        o = acc[...] / l_scr[...]
        for g in range(group):
            o_ref[rows, g * hd:(g + 1) * hd] = o[g * tq:(g + 1) * tq].astype(BF16)
        return c

    lax.fori_loop(0, seq // tq, query_tile, 0)


def _attention(proj, seq_len, n_kv, group, q_col, k_col, v_col, q_gain, k_gain, *,
               rope_tabs=None, cache=None, emit_kv=False, name):
    n = proj.shape[0]
    nseq = n // seq_len
    hd = HEAD_DIM
    tq, tk = min(ATT_TQ, seq_len), min(ATT_TK, seq_len)
    gw = group * hd
    in_specs = [pl.BlockSpec((seq_len, gw), lambda b, h: (b, q_col // gw + h)),
                pl.BlockSpec((seq_len, hd), lambda b, h: (b, k_col // hd + h)),
                pl.BlockSpec((seq_len, hd), lambda b, h: (b, v_col // hd + h)),
                pl.BlockSpec((1, hd), lambda b, h: (0, 0)),
                pl.BlockSpec((1, hd), lambda b, h: (0, 0))]
    args = [proj, proj, proj, q_gain.reshape(1, hd), k_gain.reshape(1, hd)]
    if rope_tabs is not None:
        in_specs += [pl.BlockSpec((seq_len, hd), lambda b, h: (0, 0))] * 3
        args += list(rope_tabs)
    cache_len = 0
    if cache is not None:
        kc, vc, layer = cache
        cache_len = kc.shape[3]
        in_specs += [pl.BlockSpec((1, 1, 1, cache_len, hd), lambda b, h: (b, layer, h, 0, 0))] * 2
        args += [kc, vc]
    out_shape = [jax.ShapeDtypeStruct((n, n_kv * gw), BF16)]
    out_specs = [pl.BlockSpec((seq_len, gw), lambda b, h: (b, h))]
    if emit_kv:
        out_shape += [jax.ShapeDtypeStruct((nseq, n_kv, seq_len, hd), F32)] * 2
        out_specs += [pl.BlockSpec((1, 1, seq_len, hd), lambda b, h: (b, h, 0, 0))] * 2
    m = group * tq
    res = pl.pallas_call(
        functools.partial(_attn_kernel, group=group, seq=seq_len, tq=tq, tk=tk,
                          rope=rope_tabs is not None, cache_len=cache_len, emit_kv=emit_kv),
        out_shape=out_shape,
        grid=(nseq, n_kv),
        in_specs=in_specs,
        out_specs=out_specs,
        scratch_shapes=[pltpu.VMEM((seq_len, hd), BF16), pltpu.VMEM((seq_len, hd), BF16),
                        pltpu.VMEM((m, hd), F32), pltpu.VMEM((m, hd), F32), pltpu.VMEM((m, hd), F32),
                        pltpu.VMEM((m, hd), BF16)],
        compiler_params=_cparams(2),
        name=name,
    )(*args)
    return res if emit_kv else res[0]


def _natten_kernel(q_ref, k_ref, v_ref, qn_ref, kn_ref, kc_ref, vc_ref, bias_ref, o_ref, kb, vb, *,
                   seq, width, kr, chunk):
    n_rows = seq // width
    scale = HEAD_DIM ** -0.5

    def key_chunk(i, c):
        rows = _rows(i, chunk)
        kb[rows, :] = _rms(k_ref[rows, :], kn_ref[...]).astype(BF16)
        vb[rows, :] = v_ref[rows, :].astype(BF16)
        return c

    lax.fori_loop(0, seq // chunk, key_chunk, 0)

    rg = NAT_ROWS if n_rows % NAT_ROWS == 0 else 1

    def row_group(gi, c):
        qrows = _rows(gi, rg * width)
        q = (_rms(q_ref[qrows, :], qn_ref[...]) * scale).astype(BF16)
        s_ctx = _dot_nt(q, kc_ref[0, 0, 0].astype(BF16))
        krows, s_loc = [], []
        for j in range(rg):
            r = gi * rg + j
            r0 = jnp.clip(r - kr // 2, 0, n_rows - kr)
            krows.append(pl.ds(pl.multiple_of(r0 * width, width), kr * width))
            s_loc.append(_dot_nt(q[j * width:(j + 1) * width], kb[krows[j], :]) + bias_ref[0, r - r0])
        s_loc = jnp.concatenate(s_loc, axis=0) if rg > 1 else s_loc[0]
        m = jnp.maximum(jnp.max(s_loc, axis=-1, keepdims=True), jnp.max(s_ctx, axis=-1, keepdims=True))
        p_loc = jnp.exp(s_loc - m)
        p_ctx = jnp.exp(s_ctx - m)
        denom = jnp.sum(p_loc, axis=-1, keepdims=True) + jnp.sum(p_ctx, axis=-1, keepdims=True)
        p_loc = p_loc.astype(BF16)
        o_loc = [_dot(p_loc[j * width:(j + 1) * width], vb[krows[j], :]) for j in range(rg)]
        o_loc = jnp.concatenate(o_loc, axis=0) if rg > 1 else o_loc[0]
        o = o_loc + _dot(p_ctx.astype(BF16), vc_ref[0, 0, 0].astype(BF16))
        o_ref[qrows, :] = (o / denom).astype(BF16)
        return c

    lax.fori_loop(0, n_rows // rg, row_group, 0)


def _natten_bias(rpb, width, kr):
    kc = NAT_KC
    cols = np.arange(width)
    c0 = np.clip(cols - kc // 2, 0, width - kc)
    j = np.arange(width)
    in_win = (j[None, :] >= c0[:, None]) & (j[None, :] < c0[:, None] + kc)
    ci = np.clip(j[None, :] - cols[:, None] + kc - 1, 0, 2 * kc - 2)
    off = np.arange(kr)
    ri = np.arange(kr)[None, :] - off[:, None] + NAT_KR - 1
    pick_r = np.zeros((kr * kr, 2 * NAT_KR - 1), np.float32)
    pick_r[np.arange(kr * kr), ri.reshape(-1)] = 1.0
    pick_c = np.zeros((2 * kc - 1, width * width), np.float32)
    pick_c[ci.reshape(-1), np.arange(width * width)] = 1.0
    tab = jnp.einsum('ar,hrs,sb->hab', pick_r, rpb.astype(F32), pick_c, precision=lax.Precision.HIGHEST)
    tab = tab.reshape(rpb.shape[0], kr, kr, width, width).transpose(0, 1, 3, 2, 4)
    tab = jnp.where(in_win[None, None, :, None, :], tab, MASKED)
    return tab.reshape(rpb.shape[0], kr, width, kr * width)


def _natten(proj, seq_len, q_gain, k_gain, rpb, kc, vc, layer):
    n = proj.shape[0]
    nseq = n // seq_len
    hd = HEAD_DIM
    n_heads = rpb.shape[0]
    width = GRID_W
    kr = min(NAT_KR, seq_len // width)
    bias = _natten_bias(rpb, width, kr)
    cache_len = kc.shape[3]
    chunk = min(512, seq_len)
    col = lambda c0: pl.BlockSpec((seq_len, hd), lambda b, h: (b, c0 // hd + h))
    return pl.pallas_call(
        functools.partial(_natten_kernel, seq=seq_len, width=width, kr=kr, chunk=chunk),
        out_shape=jax.ShapeDtypeStruct((n, n_heads * hd), BF16),
        grid=(nseq, n_heads),
        in_specs=[col(COL_NQ), col(COL_NK), col(COL_NV),
                  pl.BlockSpec((1, hd), lambda b, h: (0, 0)),
                  pl.BlockSpec((1, hd), lambda b, h: (0, 0)),
                  pl.BlockSpec((1, 1, 1, cache_len, hd), lambda b, h: (b, layer, h, 0, 0)),
                  pl.BlockSpec((1, 1, 1, cache_len, hd), lambda b, h: (b, layer, h, 0, 0)),
                  pl.BlockSpec((1, kr, width, kr * width), lambda b, h: (h, 0, 0, 0))],
        out_specs=pl.BlockSpec((seq_len, hd), lambda b, h: (b, h)),
        scratch_shapes=[pltpu.VMEM((seq_len, hd), BF16), pltpu.VMEM((seq_len, hd), BF16)],
        compiler_params=_cparams(2),
        name="natten_latent",
    )(proj, proj, proj, q_gain.reshape(1, hd), k_gain.reshape(1, hd), kc, vc, bias)


def _retention_kernel(q_ref, k_ref, v_ref, rg_ref, lg_ref, gn_ref, sf0_ref, sb0_ref,
                      y_ref, sf_ref, sb_ref, o_scr, kv_scr, sp_scr, *, seq, unroll):
    c = RET_CHUNK
    nc = seq // c
    hd = HEAD_DIM
    u = unroll
    lgf = lg_ref[0, 0:1, :]
    lgb = lg_ref[0, 1:2, :]
    ii = lax.broadcasted_iota(jnp.int32, (c, c), 0)
    jj = lax.broadcasted_iota(jnp.int32, (c, c), 1)
    rel = (ii - jj).astype(F32)
    pos = lax.broadcasted_iota(jnp.int32, (c, hd), 0).astype(F32)
    decay = (jnp.where(rel >= 0, jnp.exp(lgf * jnp.maximum(rel, 0.0)), 0.0)
             + jnp.where(rel <= 0, jnp.exp(lgb * jnp.maximum(-rel, 0.0)), 0.0))
    qd = jnp.concatenate([jnp.exp(lgf * (pos + 1.0)), jnp.exp(lgb * (c - pos))], axis=1)
    kd = jnp.concatenate([jnp.exp(lgf * (c - 1.0 - pos)), jnp.exp(lgb * pos)], axis=1)
    cd_f = jnp.exp(lgf * c)
    cd_b = jnp.exp(lgb * c)
    kscale = HEAD_DIM ** -0.5

    def within_chunks(gi, carry):
        rows = [_rows(gi * u + j, c) for j in range(u)]
        k = [k_ref[r, :] * kscale for r in rows]
        v = [v_ref[r, :].astype(BF16) for r in rows]
        qk = [_dot_nt(q_ref[rows[j], :].astype(BF16), k[j].astype(BF16)) for j in range(u)]
        att = [(s * decay).astype(BF16) for s in qk]
        for j in range(u):
            o_scr[rows[j], :] = _dot(att[j], v[j])
        for j in range(u):
            kk = jnp.concatenate([k[j], k[j]], axis=1) * kd
            kv_scr[gi * u + j] = _dot(kk.T.astype(BF16), v[j])
        return carry

    lax.fori_loop(0, nc // u, within_chunks, 0)

    def recurrence(t, carry):
        sf, sb = carry
        tb = nc - 1 - t
        sp_scr[t, 0:hd, :] = sf.astype(BF16)
        sp_scr[tb, hd:2 * hd, :] = sb.astype(BF16)
        return sf * cd_f + kv_scr[t, 0:hd, :], sb * cd_b + kv_scr[tb, hd:2 * hd, :]

    sf, sb = lax.fori_loop(0, nc, recurrence, (sf0_ref[0, 0, 0], sb0_ref[0, 0, 0]))
    sf_ref[0, 0] = sf
    sb_ref[0, 0] = sb

    def across_chunks(gi, carry):
        rows = [_rows(gi * u + j, c) for j in range(u)]
        o = []
        for j in range(u):
            q = q_ref[rows[j], :]
            qq = (jnp.concatenate([q, q], axis=1) * qd).astype(BF16)
            o.append(o_scr[rows[j], :] + _dot(qq, sp_scr[gi * u + j]))
        for j in range(u):
            on = o[j] * lax.rsqrt(jnp.mean(o[j] * o[j], axis=-1, keepdims=True) + EPS)
            g = rg_ref[rows[j], :]
            y_ref[rows[j], :] = ((g * _sigmoid(g)) * (on * gn_ref[...])).astype(BF16)
        return carry

    lax.fori_loop(0, nc // u, across_chunks, 0)


def _retention(proj, seq_len, log_gamma, ret_gn, s_fwd, s_bwd, layer):
    n = proj.shape[0]
    nseq = n // seq_len
    hd = HEAD_DIM
    n_heads = log_gamma.shape[0]
    nc = seq_len // RET_CHUNK
    shared = s_fwd.shape[0] == 1
    col = lambda c0: pl.BlockSpec((seq_len, hd), lambda b, h: (b, c0 // hd + h))
    state = pl.BlockSpec((1, 1, 1, hd, hd), lambda b, h: (0 if shared else b, layer, 0 if shared else h, 0, 0))
    return pl.pallas_call(
        functools.partial(_retention_kernel, seq=seq_len, unroll=next(u for u in (RET_UNROLL, 2, 1) if nc % u == 0)),
        out_shape=[jax.ShapeDtypeStruct((n, n_heads * hd), BF16),
                   jax.ShapeDtypeStruct((nseq, n_heads, hd, hd), F32),
                   jax.ShapeDtypeStruct((nseq, n_heads, hd, hd), F32)],
        grid=(nseq, n_heads),
        in_specs=[col(COL_RQ), col(COL_RK), col(COL_RV), col(COL_RG),
                  pl.BlockSpec((1, 2, hd), lambda b, h: (h, 0, 0)),
                  pl.BlockSpec((1, hd), lambda b, h: (0, h)),
                  state, state],
        out_specs=[pl.BlockSpec((seq_len, hd), lambda b, h: (b, h)),
                   pl.BlockSpec((1, 1, hd, hd), lambda b, h: (b, h, 0, 0)),
                   pl.BlockSpec((1, 1, hd, hd), lambda b, h: (b, h, 0, 0))],
        scratch_shapes=[pltpu.VMEM((seq_len, hd), F32), pltpu.VMEM((nc, 2 * hd, hd), F32),
                        pltpu.VMEM((nc, 2 * hd, hd), BF16)],
        compiler_params=_cparams(2),
        name="retention",
    )(proj, proj, proj, proj, log_gamma, ret_gn.reshape(1, n_heads * hd), s_fwd, s_bwd)


def _merge_kernel(y0_ref, y1_ref, y2_ref, y3_ref, g0_ref, g1_ref, g2_ref, g3_ref, wb_ref, wo_ref, o_ref):
    k = pl.program_id(1)
    m = g0_ref[...] * _dot(y0_ref[...], wb_ref[0, 0])
    for i, (y_ref, g_ref) in enumerate(((y1_ref, g1_ref), (y2_ref, g2_ref), (y3_ref, g3_ref)), start=1):
        m = m + g_ref[...] * _dot(y_ref[...], wb_ref[0, i])
    part = _dot(m.astype(BF16), wo_ref[0])

    @pl.when(k == 0)
    def _():
        o_ref[...] = part

    @pl.when(k > 0)
    def _():
        o_ref[...] += part


def _merge(branches, gates, w_branch, w_out, layer):
    n = gates.shape[0]
    _, _, bw, d = w_branch.shape
    tm, tk = min(MERGE_TM, n), min(MERGE_TK, d)
    nk = d // tk
    gate = lambda i: pl.BlockSpec((tm, tk), lambda t, k: (t, i * nk + k))
    return pl.pallas_call(
        _merge_kernel,
        out_shape=jax.ShapeDtypeStruct((n, d), F32),
        grid=(n // tm, nk),
        in_specs=[pl.BlockSpec((tm, bw), lambda t, k: (t, 0))] * 4
                 + [gate(0), gate(1), gate(2), gate(3),
                    pl.BlockSpec((1, N_BRANCH, bw, tk), lambda t, k: (layer, 0, 0, k)),
                    pl.BlockSpec((1, tk, d), lambda t, k: (layer, k, 0))],
        out_specs=pl.BlockSpec((tm, d), lambda t, k: (t, 0)),
        compiler_params=_cparams(2),
        name="merge_out",
    )(*branches, gates, gates, gates, gates, w_branch, w_out)


def _router_kernel(x_ref, a_ref, mod_ref, g_ref, wt_ref, b_ref, x1_ref, h_ref, eid_ref, wts_ref):
    x1 = x_ref[...] + mod_ref[0, 2:3, :] * a_ref[...]
    x1_ref[...] = x1
    y = _rms(x1, g_ref[...])
    h = y * (1.0 + mod_ref[0, 4:5, :]) + mod_ref[0, 3:4, :]
    h_ref[...] = h
    h_hi = h.astype(BF16)
    h_lo = (h - h_hi.astype(F32)).astype(BF16)
    w = wt_ref[...]
    w_hi = w.astype(BF16)
    w_lo = (w - w_hi.astype(F32)).astype(BF16)
    logits = _dot_nt(w_hi, h_hi) + (_dot_nt(w_lo, h_hi) + _dot_nt(w_hi, h_lo))
    score = _sigmoid(logits)
    sel = score + b_ref[...]
    epg = N_EXPERTS // N_GROUPS
    s = [sel[e:e + 1, :] for e in range(N_EXPERTS)]
    sc = [score[e:e + 1, :] for e in range(N_EXPERTS)]

    def group_score(vals):
        best = None
        for a in range(len(vals)):
            for b in range(a + 1, len(vals)):
                pair = vals[a] + vals[b]
                best = pair if best is None else jnp.maximum(best, pair)
        return best

    gs = [group_score(s[g * epg:(g + 1) * epg]) for g in range(N_GROUPS)]
    g_best = jnp.zeros(gs[0].shape, jnp.int32)
    best = gs[0]
    for g in range(1, N_GROUPS):
        better = gs[g] > best
        g_best = jnp.where(better, g, g_best)
        best = jnp.where(better, gs[g], best)
    in_sel, in_score = [], []
    for k in range(epg):
        v, w_ = s[k], sc[k]
        for g in range(1, N_GROUPS):
            v = jnp.where(g_best == g, s[g * epg + k], v)
            w_ = jnp.where(g_best == g, sc[g * epg + k], w_)
        in_sel.append(v)
        in_score.append(w_)
    i1 = jnp.zeros(g_best.shape, jnp.int32)
    v1, w1 = in_sel[0], in_score[0]
    for k in range(1, epg):
        better = in_sel[k] > v1
        i1 = jnp.where(better, k, i1)
        v1 = jnp.where(better, in_sel[k], v1)
        w1 = jnp.where(better, in_score[k], w1)
    i2 = jnp.zeros(g_best.shape, jnp.int32)
    v2 = jnp.full(v1.shape, -jnp.inf, F32)
    w2 = jnp.zeros(v1.shape, F32)
    for k in range(epg):
        better = (i1 != k) & (in_sel[k] > v2)
        i2 = jnp.where(better, k, i2)
        v2 = jnp.where(better, in_sel[k], v2)
        w2 = jnp.where(better, in_score[k], w2)
    total = w1 + w2
    eid_ref[...] = jnp.concatenate([g_best * epg + i1, g_best * epg + i2], axis=0)
    wts_ref[...] = jnp.concatenate([w1 / total, w2 / total], axis=0)


def _router(x, mixed, mods, mod_row0, tiles_per_seq, gain, w_router_t, b_router):
    n, d = x.shape
    tm = min(ROUTER_TM, n)
    tps = max(tiles_per_seq // tm, 1) if tiles_per_seq else n // tm
    e = w_router_t.shape[0]
    tile = pl.BlockSpec((tm, d), lambda i: (i, 0))
    return pl.pallas_call(
        _router_kernel,
        out_shape=[jax.ShapeDtypeStruct((n, d), F32),
                   jax.ShapeDtypeStruct((n, d), F32),
                   jax.ShapeDtypeStruct((TOP_K, n), jnp.int32),
                   jax.ShapeDtypeStruct((TOP_K, n), F32)],
        grid=(n // tm,),
        in_specs=[tile, tile,
                  pl.BlockSpec((1, 6, d), lambda i: (mod_row0 + i // tps, 0, 0)),
                  pl.BlockSpec((1, d), lambda i: (0, 0)),
                  pl.BlockSpec((e, d), lambda i: (0, 0)),
                  pl.BlockSpec((e, 1), lambda i: (0, 0))],
        out_specs=[tile, tile,
                   pl.BlockSpec((TOP_K, tm), lambda i: (0, i)),
                   pl.BlockSpec((TOP_K, tm), lambda i: (0, i))],
        compiler_params=_cparams(1),
        name="router",
    )(x, mixed, mods, gain.reshape(1, d), w_router_t, b_router.reshape(e, 1))


def _row_copy(src_ref, src_row, dst_ref, dst_row, sem):
    return pltpu.make_async_copy(src_ref.at[pl.ds(src_row, 1)], dst_ref.at[pl.ds(dst_row, 1)], sem)


def _dispatch_kernel(slot_ref, h_ref, buf_in_ref, buf_ref, sem, *, tt):
    del buf_in_ref

    def copies(j):
        return [_row_copy(h_ref, j, buf_ref, slot_ref[0, 0, k * tt + j], sem) for k in range(TOP_K)]

    def start(j, c):
        for k, cp in enumerate(copies(j)):
            cp.start(priority=k % 2)
        return c

    def wait(j, c):
        for cp in copies(j):
            cp.wait()
        return c

    lax.fori_loop(0, tt, start, 0, unroll=DMA_UNROLL)
    lax.fori_loop(0, tt, wait, 0, unroll=DMA_UNROLL)


def _dispatch(h, slots, buf):
    n, d = h.shape
    tt = min(MOE_TT, n)
    return pl.pallas_call(
        functools.partial(_dispatch_kernel, tt=tt),
        out_shape=jax.ShapeDtypeStruct(buf.shape, buf.dtype),
        grid=(n // tt,),
        in_specs=[pl.BlockSpec((1, 1, TOP_K * tt), lambda i: (i, 0, 0), memory_space=pltpu.SMEM),
                  pl.BlockSpec((tt, d), lambda i: (i, 0)),
                  pl.BlockSpec(memory_space=pl.ANY)],
        out_specs=pl.BlockSpec(memory_space=pl.ANY),
        scratch_shapes=[pltpu.SemaphoreType.DMA],
        input_output_aliases={2: 0},
        compiler_params=_cparams(1),
        name="moe_dispatch",
    )(slots, h, buf)


def _ffn_kernel(be_ref, nu_ref, x_ref, wg_ref, wu_ref, wd_ref, o_ref):
    del be_ref
    live = pl.program_id(0) < nu_ref[0]

    @pl.when(live)
    def _():
        x = x_ref[...].astype(BF16)
        a = _dot(x, wg_ref[0, 0])
        b = _dot(x, wu_ref[0, 0])
        o_ref[...] = _dot(((a * _sigmoid(a)) * b).astype(BF16), wd_ref[0, 0])

    @pl.when(jnp.logical_not(live))
    def _():
        o_ref[...] = jnp.zeros(o_ref.shape, F32)


def _expert_ffn(buf, block_expert, n_used, w_gate, w_up, w_down, layer):
    rows, d = buf.shape
    de = w_gate.shape[3]
    rb = MOE_ROWS
    grid_spec = pltpu.PrefetchScalarGridSpec(
        num_scalar_prefetch=2,
        grid=(rows // rb,),
        in_specs=[pl.BlockSpec((rb, d), lambda i, be, nu: (i, 0)),
                  pl.BlockSpec((1, 1, d, de), lambda i, be, nu: (layer, be[i], 0, 0)),
                  pl.BlockSpec((1, 1, d, de), lambda i, be, nu: (layer, be[i], 0, 0)),
                  pl.BlockSpec((1, 1, de, d), lambda i, be, nu: (layer, be[i], 0, 0))],
        out_specs=pl.BlockSpec((rb, d), lambda i, be, nu: (i, 0)))
    return pl.pallas_call(
        _ffn_kernel,
        out_shape=jax.ShapeDtypeStruct((rows, d), F32),
        grid_spec=grid_spec,
        compiler_params=_cparams(1),
        name="expert_ffn",
    )(block_expert, n_used, buf, w_gate, w_up, w_down)


def _combine_kernel(slot_ref, next_slot_ref, yb_ref, x_ref, w_ref, mod_ref, o_ref, gath, sems, *, tt):
    i = pl.program_id(0)
    cur = i % 2

    def copies(s_ref, half, j):
        return [_row_copy(yb_ref, s_ref[0, 0, k * tt + j], gath.at[half, k], j, sems.at[half])
                for k in range(TOP_K)]

    def fetch(s_ref, half):
        def start(j, c):
            for k, cp in enumerate(copies(s_ref, half, j)):
                cp.start(priority=k % 2)
            return c
        lax.fori_loop(0, tt, start, 0, unroll=DMA_UNROLL)

    @pl.when(i == 0)
    def _():
        fetch(slot_ref, 0)

    @pl.when(i + 1 < pl.num_programs(0))
    def _():
        fetch(next_slot_ref, 1 - cur)

    def wait(j, c):
        for cp in copies(slot_ref, cur, j):
            cp.wait()
        return c

    lax.fori_loop(0, tt, wait, 0, unroll=DMA_UNROLL)
    w = w_ref[...]
    y = w[:, 0:1] * gath[cur, 0] + w[:, 1:2] * gath[cur, 1]
    o_ref[...] = x_ref[...] + mod_ref[0, 5:6, :] * y


def _combine(x, yb, slots, wts, mods, mod_row0, tiles_per_seq):
    n, d = x.shape
    tt = min(MOE_TT, n)
    tps = max(tiles_per_seq // tt, 1) if tiles_per_seq else n // tt
    nt = n // tt
    return pl.pallas_call(
        functools.partial(_combine_kernel, tt=tt),
        out_shape=jax.ShapeDtypeStruct((n, d), F32),
        grid=(nt,),
        in_specs=[pl.BlockSpec((1, 1, TOP_K * tt), lambda i: (i, 0, 0), memory_space=pltpu.SMEM),
                  pl.BlockSpec((1, 1, TOP_K * tt), lambda i: (jnp.minimum(i + 1, nt - 1), 0, 0),
                               memory_space=pltpu.SMEM),
                  pl.BlockSpec(memory_space=pl.ANY),
                  pl.BlockSpec((tt, d), lambda i: (i, 0)),
                  pl.BlockSpec((tt, TOP_K), lambda i: (i, 0)),
                  pl.BlockSpec((1, 6, d), lambda i: (mod_row0 + i // tps, 0, 0))],
        out_specs=pl.BlockSpec((tt, d), lambda i: (i, 0)),
        scratch_shapes=[pltpu.VMEM((2, TOP_K, tt, d), F32), pltpu.SemaphoreType.DMA((2,))],
        compiler_params=_cparams(1),
        name="moe_combine",
    )(slots, slots, yb, x, wts, mods)


def _slot_blocks(slots, tt):
    k, n = slots.shape
    return slots.reshape(k, n // tt, tt).transpose(1, 0, 2).reshape(n // tt, 1, k * tt)


def _moe(xs, mixed, mods, mod_rows, seq_lens, gain, w_router, b_router, w_gate, w_up, w_down, layer,
         spare=None):
    d = xs[0].shape[1]
    w_router_t = w_router.T
    routed = [_router(x, a, mods, r0, sl, gain, w_router_t, b_router)
              for x, a, r0, sl in zip(xs, mixed, mod_rows, seq_lens)]
    flat_e = jnp.concatenate([r[2].reshape(-1) for r in routed])
    n_assign = flat_e.shape[0]
    onehot = (flat_e[:, None] == jnp.arange(N_EXPERTS, dtype=jnp.int32)[None, :]).astype(jnp.int32)
    csum = jnp.cumsum(onehot, axis=0)
    counts = csum[-1]
    padded = (counts + MOE_ROWS - 1) // MOE_ROWS * MOE_ROWS
    pend = jnp.cumsum(padded)
    slot = jnp.sum(onehot * (csum - 1 + (pend - padded)[None, :]), axis=1)
    n_blocks = -(-n_assign // MOE_ROWS) + N_EXPERTS
    if spare is not None and spare.shape == (n_blocks * MOE_ROWS, d):
        buf = spare
    else:
        buf = jnp.zeros((n_blocks * MOE_ROWS, d), F32)
    block_start = jnp.arange(n_blocks, dtype=jnp.int32) * MOE_ROWS
    block_expert = jnp.minimum(jnp.sum((pend[None, :] <= block_start[:, None]).astype(jnp.int32), axis=1),
                               N_EXPERTS - 1)
    n_used = (pend[-1:] // MOE_ROWS).astype(jnp.int32)
    slot_blocks, off = [], 0
    for r in routed:
        n = r[0].shape[0]
        sb = _slot_blocks(slot[off:off + TOP_K * n].reshape(TOP_K, n), min(MOE_TT, n))
        off += TOP_K * n
        slot_blocks.append(sb)
        buf = _dispatch(r[1], sb, buf)
    yb = _expert_ffn(buf, block_expert, n_used, w_gate, w_up, w_down, layer)
    outs = [_combine(r[0], yb, sb, r[3].T, mods, r0, sl)
            for r, sb, r0, sl in zip(routed, slot_blocks, mod_rows, seq_lens)]
    return outs, buf


def _rope_tables(n_tok):
    t = jnp.arange(n_tok)
    row = (t // GRID_W).astype(F32)
    col = (t % GRID_W).astype(F32)
    quarter = HEAD_DIM // 4
    inv = ROPE_THETA ** (-jnp.arange(quarter, dtype=F32) / quarter)
    ar = row[:, None] * inv
    ac = col[:, None] * inv
    ang = jnp.concatenate([ar, ar, ac, ac], axis=-1)
    cos, sin = jnp.cos(ang), jnp.sin(ang)
    first = (jnp.arange(HEAD_DIM) % (2 * quarter)) < quarter
    return cos, jnp.where(first, -sin, 0.0), jnp.where(first, 0.0, sin)


def kernel(x_prompt, x_sample, cache_nat_k, cache_nat_v, cache_gqa_k, cache_gqa_v, state_ret_fwd,
           state_ret_bwd, c, c_ctx, w_mod, b_mod, norm1, norm2, w_in, conv_w, nat_qn, nat_kn, nat_rpb,
           gqa_qn, gqa_kn, ret_decay_fwd, ret_decay_bwd, ret_gn, w_branch, w_out, w_router, b_router,
           w_exp_gate, w_exp_up, w_exp_down):
    batch, seq, d = x_prompt.shape
    dec_batch, dec_seq, _ = x_sample.shape
    depth = w_mod.shape[0]
    n_heads = nat_rpb.shape[1]
    n_kv = cache_gqa_k.shape[2]
    hd = HEAD_DIM

    xp = x_prompt.reshape(batch * seq, d)
    xs = x_sample.reshape(dec_batch * dec_seq, d)
    cvecs = jnp.zeros((SUBLANES, d), F32).at[0].set(c_ctx).at[1:1 + dec_batch].set(c)
    mods_all = _modulation(cvecs, w_mod, b_mod).reshape(depth, SUBLANES, 6, d)
    rope_tabs = _rope_tables(dec_seq)
    zero_state = jnp.zeros((1, depth, 1, hd, hd), F32)

    w_in_b, wb, wo = w_in.astype(BF16), w_branch.astype(BF16), w_out.astype(BF16)
    wg, wu, wd = w_exp_gate.astype(BF16), w_exp_up.astype(BF16), w_exp_down.astype(BF16)
    gate_w = w_in.shape[2] - PROJ_W

    caches = [[] for _ in range(6)]
    moe_buf = None
    for l in range(depth):
        mods = mods_all[l]
        lg = jnp.stack([jax.nn.log_sigmoid(ret_decay_fwd[l].astype(F32)),
                        jax.nn.log_sigmoid(ret_decay_bwd[l].astype(F32))], axis=1)
        lg = jnp.broadcast_to(lg[:, :, None], (lg.shape[0], 2, hd))

        pc = _norm_project(xp, mods, 0, 0, norm1[l], w_in_b, l, 0, PROJ_W, gate=False, name="in_proj_ctx")
        gc = _norm_project(xp, mods, 0, 0, norm1[l], w_in_b, l, PROJ_W, gate_w, gate=True, name="gate_proj_ctx")
        conv_c = _short_conv(pc, conv_w[l], seq)
        nat_c, nk, nv = _attention(pc, seq, n_heads, 1, COL_NQ, COL_NK, COL_NV, nat_qn[l], nat_kn[l],
                                   emit_kv=True, name="nat_ctx")
        gqa_c, gk, gv = _attention(pc, seq, n_kv, GQA_GROUP, COL_GQ, COL_GK, COL_GV, gqa_qn[l], gqa_kn[l],
                                   emit_kv=True, name="gqa_ctx")
        ret_c, s_f, s_b = _retention(pc, seq, lg, ret_gn[l], zero_state, zero_state, l)
        for lst, val in zip(caches, (nk, nv, gk, gv, s_f, s_b)):
            lst.append(val)
        mix_c = _merge((conv_c, nat_c, gqa_c, ret_c), gc, wb, wo, l)

        pl_ = _norm_project(xs, mods, 1, dec_seq, norm1[l], w_in_b, l, 0, PROJ_W, gate=False, name="in_proj_lat")
        gl = _norm_project(xs, mods, 1, dec_seq, norm1[l], w_in_b, l, PROJ_W, gate_w, gate=True,
                           name="gate_proj_lat")
        conv_l = _short_conv(pl_, conv_w[l], dec_seq)
        nat_l = _natten(pl_, dec_seq, nat_qn[l], nat_kn[l], nat_rpb[l], cache_nat_k, cache_nat_v, l)
        gqa_l = _attention(pl_, dec_seq, n_kv, GQA_GROUP, COL_GQ, COL_GK, COL_GV, gqa_qn[l], gqa_kn[l],
                           rope_tabs=rope_tabs, cache=(cache_gqa_k, cache_gqa_v, l), name="gqa_lat")
        ret_l, _, _ = _retention(pl_, dec_seq, lg, ret_gn[l], state_ret_fwd, state_ret_bwd, l)
        mix_l = _merge((conv_l, nat_l, gqa_l, ret_l), gl, wb, wo, l)

        (xp, xs), moe_buf = _moe([xp, xs], [mix_c, mix_l], mods, [0, 1], [0, dec_seq], norm2[l], w_router,
                                 b_router, wg, wu, wd, l, spare=moe_buf)

    outs = [jnp.stack(v, axis=1) for v in caches]
    return (xp.reshape(batch, seq, d), xs.reshape(dec_batch, dec_seq, d), *outs)
```

```python
import functools

import numpy as np
import jax
import jax.numpy as jnp
from jax import lax
from jax.experimental import pallas as pl
from jax.experimental.pallas import tpu as pltpu

F32 = jnp.float32
BF16 = jnp.bfloat16

GRID_W = 64
HEAD_DIM = 128
BRANCH_W = 512
N_BRANCH = 4
CONV_K = 3
NAT_KR = 8
NAT_KC = 16
GQA_GROUP = 2
RET_CHUNK = 128
ROPE_THETA = 10000.0
N_EXPERTS = 16
N_GROUPS = 4
TOP_K = 2
EPS = 1e-6
MASKED = -1e30
LOG2_E = 1.4426950408889634

COL_U, COL_BG, COL_CG = 0, 512, 1024
COL_NQ, COL_NK, COL_NV = 1536, 2048, 2560
COL_GQ, COL_GK, COL_GV = 3072, 3584, 3840
COL_RQ, COL_RK, COL_RV, COL_RG = 4096, 4608, 5120, 5632
PROJ_W = 6144

V7X_VMEM_LIMIT_BYTES = 60 * 1024 * 1024
SUBLANES = 8
LANES = 128

PROJ_TM, PROJ_TN = 1024, 2048
NORM_ROWS = 256
MERGE_TM, MERGE_TK = 1024, 512
ROUTER_TM = 512
MOE_ROWS = 256
MOE_TT = 256
CONV_TT = 1024
ATT_TQ, ATT_TK = 1024, 512
MOD_TN = 1024
NAT_ROWS = 16
RET_UNROLL = 4
DMA_UNROLL = 8


def _cparams(n_axes):
    return pltpu.CompilerParams(dimension_semantics=("arbitrary",) * n_axes,
                                vmem_limit_bytes=V7X_VMEM_LIMIT_BYTES)


def _sigmoid(x):
    return 1.0 / (1.0 + jnp.exp(-x))


def _dot(a, b):
    return jnp.dot(a, b, preferred_element_type=F32)


def _dot_nt(a, b):
    return lax.dot_general(a, b, (((1,), (1,)), ((), ())), preferred_element_type=F32)


def _rms(x, gain):
    return x * lax.rsqrt(jnp.mean(x * x, axis=-1, keepdims=True) + EPS) * gain


def _rows(i, n):
    return pl.ds(pl.multiple_of(i * n, n), n)


def _mod_kernel(c_ref, w_ref, b_ref, o_ref):
    c = c_ref[...]
    a = (c * _sigmoid(c)).astype(BF16)
    o_ref[0] = _dot(a, w_ref[0].astype(BF16)) + b_ref[0]


def _modulation(cvecs, w_mod, b_mod):
    depth, d, n = w_mod.shape
    tn = min(MOD_TN, n)
    return pl.pallas_call(
        _mod_kernel,
        out_shape=jax.ShapeDtypeStruct((depth, cvecs.shape[0], n), F32),
        grid=(depth, n // tn),
        in_specs=[pl.BlockSpec(cvecs.shape, lambda l, j: (0, 0)),
                  pl.BlockSpec((1, d, tn), lambda l, j: (l, 0, j)),
                  pl.BlockSpec((1, 1, tn), lambda l, j: (l, 0, j))],
        out_specs=pl.BlockSpec((1, cvecs.shape[0], tn), lambda l, j: (l, 0, j)),
        compiler_params=_cparams(2),
        name="modulation",
    )(cvecs, w_mod, b_mod.reshape(depth, 1, n))


def _normproj_kernel(x_ref, mod_ref, g_ref, w_ref, o_ref, h_scr, *, gate):
    @pl.when(pl.program_id(1) == 0)
    def _():
        gain = g_ref[...] * (1.0 + mod_ref[0, 1:2, :])
        shift = mod_ref[0, 0:1, :]

        def chunk(i, c):
            rows = _rows(i, NORM_ROWS)
            x = x_ref[rows, :]
            y = x * lax.rsqrt(jnp.mean(x * x, axis=-1, keepdims=True) + EPS)
            h_scr[rows, :] = (y * gain + shift).astype(BF16)
            return c

        lax.fori_loop(0, x_ref.shape[0] // NORM_ROWS, chunk, 0)

    acc = _dot(h_scr[...], w_ref[0])
    o_ref[...] = _sigmoid(acc) if gate else acc


def _norm_project(x, mods, mod_row0, tiles_per_seq, gain, w_all, layer, col0, nout, *, gate, name):
    n, d = x.shape
    tm, tn = min(PROJ_TM, n), min(PROJ_TN, nout)
    assert col0 % tn == 0 and nout % tn == 0
    tps = max(tiles_per_seq // tm, 1) if tiles_per_seq else n // tm
    return pl.pallas_call(
        functools.partial(_normproj_kernel, gate=gate),
        out_shape=jax.ShapeDtypeStruct((n, nout), F32),
        grid=(n // tm, nout // tn),
        in_specs=[pl.BlockSpec((tm, d), lambda i, j: (i, 0)),
                  pl.BlockSpec((1, 6, d), lambda i, j: (mod_row0 + i // tps, 0, 0)),
                  pl.BlockSpec((1, d), lambda i, j: (0, 0)),
                  pl.BlockSpec((1, d, tn), lambda i, j: (layer, 0, col0 // tn + j))],
        out_specs=pl.BlockSpec((tm, tn), lambda i, j: (i, j)),
        scratch_shapes=[pltpu.VMEM((tm, d), BF16)],
        compiler_params=_cparams(2),
        name=name,
    )(x, mods, gain.reshape(1, d), w_all)


def _conv_kernel(u_ref, bg_ref, cg_ref, up_ref, cp_ref, un_ref, cn_ref, w_ref, o_ref, *, tt, nt):
    t = pl.program_id(1)
    z = cg_ref[...] * u_ref[...]
    last = SUBLANES - 1
    z_before = jnp.where(t > 0, cp_ref[last:last + 1, :] * up_ref[last:last + 1, :], 0.0)
    z_after = jnp.where(t < nt - 1, cn_ref[0:1, :] * un_ref[0:1, :], 0.0)
    ri = lax.broadcasted_iota(jnp.int32, z.shape, 0)
    z_prev = jnp.where(ri == 0, z_before, pltpu.roll(z, 1, 0))
    z_next = jnp.where(ri == tt - 1, z_after, pltpu.roll(z, tt - 1, 0))
    w = w_ref[...]
    o_ref[...] = (bg_ref[...] * (w[0:1, :] * z_prev + w[1:2, :] * z + w[2:3, :] * z_next)).astype(BF16)


def _short_conv(proj, conv_w, seq_len):
    n = proj.shape[0]
    cw = conv_w.shape[1]
    nseq = n // seq_len
    tt = min(CONV_TT, seq_len)
    nt = seq_len // tt
    cu, cb, cc = COL_U // cw, COL_BG // cw, COL_CG // cw
    nblk8 = n // SUBLANES

    def main(col):
        return pl.BlockSpec((tt, cw), lambda b, t: (b * nt + t, col))

    def before(col):
        return pl.BlockSpec((SUBLANES, cw),
                            lambda b, t: (jnp.maximum((b * nt + t) * (tt // SUBLANES) - 1, 0), col))

    def after(col):
        return pl.BlockSpec((SUBLANES, cw),
                            lambda b, t: (jnp.minimum((b * nt + t + 1) * (tt // SUBLANES), nblk8 - 1), col))

    return pl.pallas_call(
        functools.partial(_conv_kernel, tt=tt, nt=nt),
        out_shape=jax.ShapeDtypeStruct((n, cw), BF16),
        grid=(nseq, nt),
        in_specs=[main(cu), main(cb), main(cc), before(cu), before(cc), after(cu), after(cc),
                  pl.BlockSpec(conv_w.shape, lambda b, t: (0, 0))],
        out_specs=pl.BlockSpec((tt, cw), lambda b, t: (b * nt + t, 0)),
        compiler_params=_cparams(2),
        name="short_conv",
    )(proj, proj, proj, proj, proj, proj, proj, conv_w)


def _attn_kernel(*refs, group, seq, tq, tk, rope, cache_len, emit_kv):
    it = iter(refs)
    q_ref, k_ref, v_ref, qn_ref, kn_ref = (next(it) for _ in range(5))
    if rope:
        cos_ref, sin_lo_ref, sin_hi_ref = (next(it) for _ in range(3))
    if cache_len:
        kc_ref, vc_ref = next(it), next(it)
    o_ref = next(it)
    if emit_kv:
        ko_ref, vo_ref = next(it), next(it)
    kb, vb, m_scr, l_scr, acc, q_scr = (next(it) for _ in range(6))
    scale = HEAD_DIM ** -0.5 * LOG2_E
    hd = HEAD_DIM

    def prep(x, gain_ref, rows):
        y = _rms(x, gain_ref[...])
        if rope:
            y = (y * cos_ref[rows, :] + pltpu.roll(y, hd - hd // 4, 1) * sin_lo_ref[rows, :]
                 + pltpu.roll(y, hd // 4, 1) * sin_hi_ref[rows, :])
        return y

    def key_tile(kt, c):
        rows = _rows(kt, tk)
        kn = prep(k_ref[rows, :], kn_ref, rows)
        v = v_ref[rows, :]
        kb[rows, :] = kn.astype(BF16)
        vb[rows, :] = v.astype(BF16)
        if emit_kv:
            ko_ref[0, 0, rows, :] = kn
            vo_ref[0, 0, rows, :] = v
        return c

    lax.fori_loop(0, seq // tk, key_tile, 0)

    def softmax_step(s, vblk):
        m_prev = m_scr[...]
        m_new = jnp.maximum(m_prev, jnp.max(s, axis=-1, keepdims=True))
        alpha = jnp.exp2(m_prev - m_new)
        p = jnp.exp2(s - jnp.concatenate([m_new] * (s.shape[1] // hd), axis=1))
        l_scr[...] = alpha * l_scr[...] + jnp.sum(p, axis=-1, keepdims=True)
        acc[...] = alpha * acc[...] + _dot(p.astype(BF16), vblk)
        m_scr[...] = m_new

    def query_tile(qt, c):
        rows = _rows(qt, tq)
        for g in range(group):
            q_scr[g * tq:(g + 1) * tq, :] = (
                prep(q_ref[rows, g * hd:(g + 1) * hd], qn_ref, rows) * scale).astype(BF16)
        m_scr[...] = jnp.full(m_scr.shape, MASKED, F32)
        l_scr[...] = jnp.zeros(l_scr.shape, F32)
        acc[...] = jnp.zeros(acc.shape, F32)
        n_kv = seq // tk

        def scores(kt):
            return _dot_nt(q_scr[...], kb[_rows(kt, tk), :])

        def kv_tile(kt, c2):
            softmax_step(scores(kt), vb[_rows(kt, tk), :])
            return c2

        lax.fori_loop(0, n_kv, kv_tile, 0)
        if cache_len:
            softmax_step(_dot_nt(q_scr[...], kc_ref[0, 0, 0].astype(BF16)), vc_ref[0, 0, 0].astype(BF16))
        o = acc[...] / l_scr[...]
        for g in range(group):
            o_ref[rows, g * hd:(g + 1) * hd] = o[g * tq:(g + 1) * tq].astype(BF16)
        return c

    lax.fori_loop(0, seq // tq, query_tile, 0)


def _attention(proj, seq_len, n_kv, group, q_col, k_col, v_col, q_gain, k_gain, *,
               rope_tabs=None, cache=None, emit_kv=False, name):
    n = proj.shape[0]
    nseq = n // seq_len
    hd = HEAD_DIM
    tq, tk = min(ATT_TQ, seq_len), min(ATT_TK, seq_len)
    gw = group * hd
    in_specs = [pl.BlockSpec((seq_len, gw), lambda b, h: (b, q_col // gw + h)),
                pl.BlockSpec((seq_len, hd), lambda b, h: (b, k_col // hd + h)),
                pl.BlockSpec((seq_len, hd), lambda b, h: (b, v_col // hd + h)),
                pl.BlockSpec((1, hd), lambda b, h: (0, 0)),
                pl.BlockSpec((1, hd), lambda b, h: (0, 0))]
    args = [proj, proj, proj, q_gain.reshape(1, hd), k_gain.reshape(1, hd)]
    if rope_tabs is not None:
        in_specs += [pl.BlockSpec((seq_len, hd), lambda b, h: (0, 0))] * 3
        args += list(rope_tabs)
    cache_len = 0
    if cache is not None:
        kc, vc, layer = cache
        cache_len = kc.shape[3]
        in_specs += [pl.BlockSpec((1, 1, 1, cache_len, hd), lambda b, h: (b, layer, h, 0, 0))] * 2
        args += [kc, vc]
    out_shape = [jax.ShapeDtypeStruct((n, n_kv * gw), BF16)]
    out_specs = [pl.BlockSpec((seq_len, gw), lambda b, h: (b, h))]
    if emit_kv:
        out_shape += [jax.ShapeDtypeStruct((nseq, n_kv, seq_len, hd), F32)] * 2
        out_specs += [pl.BlockSpec((1, 1, seq_len, hd), lambda b, h: (b, h, 0, 0))] * 2
    m = group * tq
    res = pl.pallas_call(
        functools.partial(_attn_kernel, group=group, seq=seq_len, tq=tq, tk=tk,
                          rope=rope_tabs is not None, cache_len=cache_len, emit_kv=emit_kv),
        out_shape=out_shape,
        grid=(nseq, n_kv),
        in_specs=in_specs,
        out_specs=out_specs,
        scratch_shapes=[pltpu.VMEM((seq_len, hd), BF16), pltpu.VMEM((seq_len, hd), BF16),
                        pltpu.VMEM((m, hd), F32), pltpu.VMEM((m, hd), F32), pltpu.VMEM((m, hd), F32),
                        pltpu.VMEM((m, hd), BF16)],
        compiler_params=_cparams(2),
        name=name,
    )(*args)
    return res if emit_kv else res[0]


def _natten_kernel(q_ref, k_ref, v_ref, qn_ref, kn_ref, kc_ref, vc_ref, bias_ref, o_ref, kb, vb, *,
                   seq, width, kr, chunk):
    n_rows = seq // width
    scale = HEAD_DIM ** -0.5

    def key_chunk(i, c):
        rows = _rows(i, chunk)
        kb[rows, :] = _rms(k_ref[rows, :], kn_ref[...]).astype(BF16)
        vb[rows, :] = v_ref[rows, :].astype(BF16)
        return c

    lax.fori_loop(0, seq // chunk, key_chunk, 0)

    rg = NAT_ROWS if n_rows % NAT_ROWS == 0 else 1

    def row_group(gi, c):
        qrows = _rows(gi, rg * width)
        q = (_rms(q_ref[qrows, :], qn_ref[...]) * scale).astype(BF16)
        s_ctx = _dot_nt(q, kc_ref[0, 0, 0].astype(BF16))
        krows, s_loc = [], []
        for j in range(rg):
            r = gi * rg + j
            r0 = jnp.clip(r - kr // 2, 0, n_rows - kr)
            krows.append(pl.ds(pl.multiple_of(r0 * width, width), kr * width))
            s_loc.append(_dot_nt(q[j * width:(j + 1) * width], kb[krows[j], :]) + bias_ref[0, r - r0])
        s_loc = jnp.concatenate(s_loc, axis=0) if rg > 1 else s_loc[0]
        m = jnp.maximum(jnp.max(s_loc, axis=-1, keepdims=True), jnp.max(s_ctx, axis=-1, keepdims=True))
        p_loc = jnp.exp(s_loc - m)
        p_ctx = jnp.exp(s_ctx - m)
        denom = jnp.sum(p_loc, axis=-1, keepdims=True) + jnp.sum(p_ctx, axis=-1, keepdims=True)
        p_loc = p_loc.astype(BF16)
        o_loc = [_dot(p_loc[j * width:(j + 1) * width], vb[krows[j], :]) for j in range(rg)]
        o_loc = jnp.concatenate(o_loc, axis=0) if rg > 1 else o_loc[0]
        o = o_loc + _dot(p_ctx.astype(BF16), vc_ref[0, 0, 0].astype(BF16))
        o_ref[qrows, :] = (o / denom).astype(BF16)
        return c

    lax.fori_loop(0, n_rows // rg, row_group, 0)


def _natten_bias(rpb, width, kr):
    kc = NAT_KC
    cols = np.arange(width)
    c0 = np.clip(cols - kc // 2, 0, width - kc)
    j = np.arange(width)
    in_win = (j[None, :] >= c0[:, None]) & (j[None, :] < c0[:, None] + kc)
    ci = np.clip(j[None, :] - cols[:, None] + kc - 1, 0, 2 * kc - 2)
    off = np.arange(kr)
    ri = np.arange(kr)[None, :] - off[:, None] + NAT_KR - 1
    pick_r = np.zeros((kr * kr, 2 * NAT_KR - 1), np.float32)
    pick_r[np.arange(kr * kr), ri.reshape(-1)] = 1.0
    pick_c = np.zeros((2 * kc - 1, width * width), np.float32)
    pick_c[ci.reshape(-1), np.arange(width * width)] = 1.0
    tab = jnp.einsum('ar,hrs,sb->hab', pick_r, rpb.astype(F32), pick_c, precision=lax.Precision.HIGHEST)
    tab = tab.reshape(rpb.shape[0], kr, kr, width, width).transpose(0, 1, 3, 2, 4)
    tab = jnp.where(in_win[None, None, :, None, :], tab, MASKED)
    return tab.reshape(rpb.shape[0], kr, width, kr * width)


def _natten(proj, seq_len, q_gain, k_gain, rpb, kc, vc, layer):
    n = proj.shape[0]
    nseq = n // seq_len
    hd = HEAD_DIM
    n_heads = rpb.shape[0]
    width = GRID_W
    kr = min(NAT_KR, seq_len // width)
    bias = _natten_bias(rpb, width, kr)
    cache_len = kc.shape[3]
    chunk = min(512, seq_len)
    col = lambda c0: pl.BlockSpec((seq_len, hd), lambda b, h: (b, c0 // hd + h))
    return pl.pallas_call(
        functools.partial(_natten_kernel, seq=seq_len, width=width, kr=kr, chunk=chunk),
        out_shape=jax.ShapeDtypeStruct((n, n_heads * hd), BF16),
        grid=(nseq, n_heads),
        in_specs=[col(COL_NQ), col(COL_NK), col(COL_NV),
                  pl.BlockSpec((1, hd), lambda b, h: (0, 0)),
                  pl.BlockSpec((1, hd), lambda b, h: (0, 0)),
                  pl.BlockSpec((1, 1, 1, cache_len, hd), lambda b, h: (b, layer, h, 0, 0)),
                  pl.BlockSpec((1, 1, 1, cache_len, hd), lambda b, h: (b, layer, h, 0, 0)),
                  pl.BlockSpec((1, kr, width, kr * width), lambda b, h: (h, 0, 0, 0))],
        out_specs=pl.BlockSpec((seq_len, hd), lambda b, h: (b, h)),
        scratch_shapes=[pltpu.VMEM((seq_len, hd), BF16), pltpu.VMEM((seq_len, hd), BF16)],
        compiler_params=_cparams(2),
        name="natten_latent",
    )(proj, proj, proj, q_gain.reshape(1, hd), k_gain.reshape(1, hd), kc, vc, bias)


def _retention_kernel(q_ref, k_ref, v_ref, rg_ref, lg_ref, gn_ref, sf0_ref, sb0_ref,
                      y_ref, sf_ref, sb_ref, o_scr, kv_scr, sp_scr, *, seq, unroll):
    c = RET_CHUNK
    nc = seq // c
    hd = HEAD_DIM
    u = unroll
    lgf = lg_ref[0, 0:1, :]
    lgb = lg_ref[0, 1:2, :]
    ii = lax.broadcasted_iota(jnp.int32, (c, c), 0)
    jj = lax.broadcasted_iota(jnp.int32, (c, c), 1)
    rel = (ii - jj).astype(F32)
    pos = lax.broadcasted_iota(jnp.int32, (c, hd), 0).astype(F32)
    decay = (jnp.where(rel >= 0, jnp.exp(lgf * jnp.maximum(rel, 0.0)), 0.0)
             + jnp.where(rel <= 0, jnp.exp(lgb * jnp.maximum(-rel, 0.0)), 0.0))
    qd = jnp.concatenate([jnp.exp(lgf * (pos + 1.0)), jnp.exp(lgb * (c - pos))], axis=1)
    kd = jnp.concatenate([jnp.exp(lgf * (c - 1.0 - pos)), jnp.exp(lgb * pos)], axis=1)
    cd_f = jnp.exp(lgf * c)
    cd_b = jnp.exp(lgb * c)
    kscale = HEAD_DIM ** -0.5

    def within_chunks(gi, carry):
        rows = [_rows(gi * u + j, c) for j in range(u)]
        k = [k_ref[r, :] * kscale for r in rows]
        v = [v_ref[r, :].astype(BF16) for r in rows]
        qk = [_dot_nt(q_ref[rows[j], :].astype(BF16), k[j].astype(BF16)) for j in range(u)]
        att = [(s * decay).astype(BF16) for s in qk]
        for j in range(u):
            o_scr[rows[j], :] = _dot(att[j], v[j])
        for j in range(u):
            kk = jnp.concatenate([k[j], k[j]], axis=1) * kd
            kv_scr[gi * u + j] = _dot(kk.T.astype(BF16), v[j])
        return carry

    lax.fori_loop(0, nc // u, within_chunks, 0)

    def recurrence(t, carry):
        sf, sb = carry
        tb = nc - 1 - t
        sp_scr[t, 0:hd, :] = sf.astype(BF16)
        sp_scr[tb, hd:2 * hd, :] = sb.astype(BF16)
        return sf * cd_f + kv_scr[t, 0:hd, :], sb * cd_b + kv_scr[tb, hd:2 * hd, :]

    sf, sb = lax.fori_loop(0, nc, recurrence, (sf0_ref[0, 0, 0], sb0_ref[0, 0, 0]))
    sf_ref[0, 0] = sf
    sb_ref[0, 0] = sb

    def across_chunks(gi, carry):
        rows = [_rows(gi * u + j, c) for j in range(u)]
        o = []
        for j in range(u):
            q = q_ref[rows[j], :]
            qq = (jnp.concatenate([q, q], axis=1) * qd).astype(BF16)
            o.append(o_scr[rows[j], :] + _dot(qq, sp_scr[gi * u + j]))
        for j in range(u):
            on = o[j] * lax.rsqrt(jnp.mean(o[j] * o[j], axis=-1, keepdims=True) + EPS)
            g = rg_ref[rows[j], :]
            y_ref[rows[j], :] = ((g * _sigmoid(g)) * (on * gn_ref[...])).astype(BF16)
        return carry

    lax.fori_loop(0, nc // u, across_chunks, 0)


def _retention(proj, seq_len, log_gamma, ret_gn, s_fwd, s_bwd, layer):
    n = proj.shape[0]
    nseq = n // seq_len
    hd = HEAD_DIM
    n_heads = log_gamma.shape[0]
    nc = seq_len // RET_CHUNK
    shared = s_fwd.shape[0] == 1
    col = lambda c0: pl.BlockSpec((seq_len, hd), lambda b, h: (b, c0 // hd + h))
    state = pl.BlockSpec((1, 1, 1, hd, hd), lambda b, h: (0 if shared else b, layer, 0 if shared else h, 0, 0))
    return pl.pallas_call(
        functools.partial(_retention_kernel, seq=seq_len, unroll=next(u for u in (RET_UNROLL, 2, 1) if nc % u == 0)),
        out_shape=[jax.ShapeDtypeStruct((n, n_heads * hd), BF16),
                   jax.ShapeDtypeStruct((nseq, n_heads, hd, hd), F32),
                   jax.ShapeDtypeStruct((nseq, n_heads, hd, hd), F32)],
        grid=(nseq, n_heads),
        in_specs=[col(COL_RQ), col(COL_RK), col(COL_RV), col(COL_RG),
                  pl.BlockSpec((1, 2, hd), lambda b, h: (h, 0, 0)),
                  pl.BlockSpec((1, hd), lambda b, h: (0, h)),
                  state, state],
        out_specs=[pl.BlockSpec((seq_len, hd), lambda b, h: (b, h)),
                   pl.BlockSpec((1, 1, hd, hd), lambda b, h: (b, h, 0, 0)),
                   pl.BlockSpec((1, 1, hd, hd), lambda b, h: (b, h, 0, 0))],
        scratch_shapes=[pltpu.VMEM((seq_len, hd), F32), pltpu.VMEM((nc, 2 * hd, hd), F32),
                        pltpu.VMEM((nc, 2 * hd, hd), BF16)],
        compiler_params=_cparams(2),
        name="retention",
    )(proj, proj, proj, proj, log_gamma, ret_gn.reshape(1, n_heads * hd), s_fwd, s_bwd)


def _merge_kernel(y0_ref, y1_ref, y2_ref, y3_ref, g0_ref, g1_ref, g2_ref, g3_ref, wb_ref, wo_ref, o_ref):
    k = pl.program_id(1)
    m = g0_ref[...] * _dot(y0_ref[...], wb_ref[0, 0])
    for i, (y_ref, g_ref) in enumerate(((y1_ref, g1_ref), (y2_ref, g2_ref), (y3_ref, g3_ref)), start=1):
        m = m + g_ref[...] * _dot(y_ref[...], wb_ref[0, i])
    part = _dot(m.astype(BF16), wo_ref[0])

    @pl.when(k == 0)
    def _():
        o_ref[...] = part

    @pl.when(k > 0)
    def _():
        o_ref[...] += part


def _merge(branches, gates, w_branch, w_out, layer):
    n = gates.shape[0]
    _, _, bw, d = w_branch.shape
    tm, tk = min(MERGE_TM, n), min(MERGE_TK, d)
    nk = d // tk
    gate = lambda i: pl.BlockSpec((tm, tk), lambda t, k: (t, i * nk + k))
    return pl.pallas_call(
        _merge_kernel,
        out_shape=jax.ShapeDtypeStruct((n, d), F32),
        grid=(n // tm, nk),
        in_specs=[pl.BlockSpec((tm, bw), lambda t, k: (t, 0))] * 4
                 + [gate(0), gate(1), gate(2), gate(3),
                    pl.BlockSpec((1, N_BRANCH, bw, tk), lambda t, k: (layer, 0, 0, k)),
                    pl.BlockSpec((1, tk, d), lambda t, k: (layer, k, 0))],
        out_specs=pl.BlockSpec((tm, d), lambda t, k: (t, 0)),
        compiler_params=_cparams(2),
        name="merge_out",
    )(*branches, gates, gates, gates, gates, w_branch, w_out)


def _router_kernel(x_ref, a_ref, mod_ref, g_ref, wt_ref, b_ref, h_ref, eid_ref, wts_ref):
    x1 = x_ref[...] + mod_ref[0, 2:3, :] * a_ref[...]
    y = _rms(x1, g_ref[...])
    h = y * (1.0 + mod_ref[0, 4:5, :]) + mod_ref[0, 3:4, :]
    h_ref[...] = h
    h_hi = h.astype(BF16)
    h_lo = (h - h_hi.astype(F32)).astype(BF16)
    w = wt_ref[...]
    w_hi = w.astype(BF16)
    w_lo = (w - w_hi.astype(F32)).astype(BF16)
    logits = _dot_nt(w_hi, h_hi) + (_dot_nt(w_lo, h_hi) + _dot_nt(w_hi, h_lo))
    score = _sigmoid(logits)
    sel = score + b_ref[...]
    epg = N_EXPERTS // N_GROUPS
    s = [sel[e:e + 1, :] for e in range(N_EXPERTS)]
    sc = [score[e:e + 1, :] for e in range(N_EXPERTS)]

    def group_score(vals):
        best = None
        for a in range(len(vals)):
            for b in range(a + 1, len(vals)):
                pair = vals[a] + vals[b]
                best = pair if best is None else jnp.maximum(best, pair)
        return best

    gs = [group_score(s[g * epg:(g + 1) * epg]) for g in range(N_GROUPS)]
    g_best = jnp.zeros(gs[0].shape, jnp.int32)
    best = gs[0]
    for g in range(1, N_GROUPS):
        better = gs[g] > best
        g_best = jnp.where(better, g, g_best)
        best = jnp.where(better, gs[g], best)
    in_sel, in_score = [], []
    for k in range(epg):
        v, w_ = s[k], sc[k]
        for g in range(1, N_GROUPS):
            v = jnp.where(g_best == g, s[g * epg + k], v)
            w_ = jnp.where(g_best == g, sc[g * epg + k], w_)
        in_sel.append(v)
        in_score.append(w_)
    i1 = jnp.zeros(g_best.shape, jnp.int32)
    v1, w1 = in_sel[0], in_score[0]
    for k in range(1, epg):
        better = in_sel[k] > v1
        i1 = jnp.where(better, k, i1)
        v1 = jnp.where(better, in_sel[k], v1)
        w1 = jnp.where(better, in_score[k], w1)
    i2 = jnp.zeros(g_best.shape, jnp.int32)
    v2 = jnp.full(v1.shape, -jnp.inf, F32)
    w2 = jnp.zeros(v1.shape, F32)
    for k in range(epg):
        better = (i1 != k) & (in_sel[k] > v2)
        i2 = jnp.where(better, k, i2)
        v2 = jnp.where(better, in_sel[k], v2)
        w2 = jnp.where(better, in_score[k], w2)
    total = w1 + w2
    eid_ref[...] = jnp.concatenate([g_best * epg + i1, g_best * epg + i2], axis=0)
    wts_ref[...] = jnp.concatenate([w1 / total, w2 / total], axis=0)


def _router(x, mixed, mods, mod_row0, tiles_per_seq, gain, w_router_t, b_router):
    n, d = x.shape
    tm = min(ROUTER_TM, n)
    tps = max(tiles_per_seq // tm, 1) if tiles_per_seq else n // tm
    e = w_router_t.shape[0]
    tile = pl.BlockSpec((tm, d), lambda i: (i, 0))
    return pl.pallas_call(
        _router_kernel,
        out_shape=[jax.ShapeDtypeStruct((n, d), F32),
                   jax.ShapeDtypeStruct((TOP_K, n), jnp.int32),
                   jax.ShapeDtypeStruct((TOP_K, n), F32)],
        grid=(n // tm,),
        in_specs=[tile, tile,
                  pl.BlockSpec((1, 6, d), lambda i: (mod_row0 + i // tps, 0, 0)),
                  pl.BlockSpec((1, d), lambda i: (0, 0)),
                  pl.BlockSpec((e, d), lambda i: (0, 0)),
                  pl.BlockSpec((e, 1), lambda i: (0, 0))],
        out_specs=[tile,
                   pl.BlockSpec((TOP_K, tm), lambda i: (0, i)),
                   pl.BlockSpec((TOP_K, tm), lambda i: (0, i))],
        compiler_params=_cparams(1),
        name="router",
    )(x, mixed, mods, gain.reshape(1, d), w_router_t, b_router.reshape(e, 1))


def _row_copy(src_ref, src_row, dst_ref, dst_row, sem):
    return pltpu.make_async_copy(src_ref.at[pl.ds(src_row, 1)], dst_ref.at[pl.ds(dst_row, 1)], sem)


def _dispatch_kernel(slot_ref, h_ref, buf_in_ref, buf_ref, sem, *, tt):
    del buf_in_ref

    def copies(j):
        return [_row_copy(h_ref, j, buf_ref, slot_ref[0, 0, k * tt + j], sem) for k in range(TOP_K)]

    def start(j, c):
        for k, cp in enumerate(copies(j)):
            cp.start(priority=k % 2)
        return c

    def wait(j, c):
        for cp in copies(j):
            cp.wait()
        return c

    lax.fori_loop(0, tt, start, 0, unroll=DMA_UNROLL)
    lax.fori_loop(0, tt, wait, 0, unroll=DMA_UNROLL)


def _dispatch(h, slots, buf):
    n, d = h.shape
    tt = min(MOE_TT, n)
    return pl.pallas_call(
        functools.partial(_dispatch_kernel, tt=tt),
        out_shape=jax.ShapeDtypeStruct(buf.shape, buf.dtype),
        grid=(n // tt,),
        in_specs=[pl.BlockSpec((1, 1, TOP_K * tt), lambda i: (i, 0, 0), memory_space=pltpu.SMEM),
                  pl.BlockSpec((tt, d), lambda i: (i, 0)),
                  pl.BlockSpec(memory_space=pl.ANY)],
        out_specs=pl.BlockSpec(memory_space=pl.ANY),
        scratch_shapes=[pltpu.SemaphoreType.DMA],
        input_output_aliases={2: 0},
        compiler_params=_cparams(1),
        name="moe_dispatch",
    )(slots, h, buf)


def _ffn_kernel(be_ref, nu_ref, x_ref, wg_ref, wu_ref, wd_ref, o_ref):
    del be_ref
    live = pl.program_id(0) < nu_ref[0]

    @pl.when(live)
    def _():
        x = x_ref[...].astype(BF16)
        a = _dot(x, wg_ref[0, 0])
        b = _dot(x, wu_ref[0, 0])
        o_ref[...] = _dot(((a * _sigmoid(a)) * b).astype(BF16), wd_ref[0, 0])

    @pl.when(jnp.logical_not(live))
    def _():
        o_ref[...] = jnp.zeros(o_ref.shape, F32)


def _expert_ffn(buf, block_expert, n_used, w_gate, w_up, w_down, layer):
    rows, d = buf.shape
    de = w_gate.shape[3]
    rb = MOE_ROWS
    grid_spec = pltpu.PrefetchScalarGridSpec(
        num_scalar_prefetch=2,
        grid=(rows // rb,),
        in_specs=[pl.BlockSpec((rb, d), lambda i, be, nu: (i, 0)),
                  pl.BlockSpec((1, 1, d, de), lambda i, be, nu: (layer, be[i], 0, 0)),
                  pl.BlockSpec((1, 1, d, de), lambda i, be, nu: (layer, be[i], 0, 0)),
                  pl.BlockSpec((1, 1, de, d), lambda i, be, nu: (layer, be[i], 0, 0))],
        out_specs=pl.BlockSpec((rb, d), lambda i, be, nu: (i, 0)))
    return pl.pallas_call(
        _ffn_kernel,
        out_shape=jax.ShapeDtypeStruct((rows, d), F32),
        grid_spec=grid_spec,
        compiler_params=_cparams(1),
        name="expert_ffn",
    )(block_expert, n_used, buf, w_gate, w_up, w_down)


def _combine_kernel(slot_ref, next_slot_ref, yb_ref, x_ref, a_ref, w_ref, mod_ref, o_ref, gath, sems, *, tt):
    i = pl.program_id(0)
    cur = i % 2

    def copies(s_ref, half, j):
        return [_row_copy(yb_ref, s_ref[0, 0, k * tt + j], gath.at[half, k], j, sems.at[half])
                for k in range(TOP_K)]

    def fetch(s_ref, half):
        def start(j, c):
            for k, cp in enumerate(copies(s_ref, half, j)):
                cp.start(priority=k % 2)
            return c
        lax.fori_loop(0, tt, start, 0, unroll=DMA_UNROLL)

    @pl.when(i == 0)
    def _():
        fetch(slot_ref, 0)

    @pl.when(i + 1 < pl.num_programs(0))
    def _():
        fetch(next_slot_ref, 1 - cur)

    def wait(j, c):
        for cp in copies(slot_ref, cur, j):
            cp.wait()
        return c

    lax.fori_loop(0, tt, wait, 0, unroll=DMA_UNROLL)
    w = w_ref[...]
    y = w[:, 0:1] * gath[cur, 0] + w[:, 1:2] * gath[cur, 1]
    x1 = x_ref[...] + mod_ref[0, 2:3, :] * a_ref[...]
    o_ref[...] = x1 + mod_ref[0, 5:6, :] * y


def _combine(x, mixed, yb, slots, wts, mods, mod_row0, tiles_per_seq):
    n, d = x.shape
    tt = min(MOE_TT, n)
    tps = max(tiles_per_seq // tt, 1) if tiles_per_seq else n // tt
    nt = n // tt
    return pl.pallas_call(
        functools.partial(_combine_kernel, tt=tt),
        out_shape=jax.ShapeDtypeStruct((n, d), F32),
        grid=(nt,),
        in_specs=[pl.BlockSpec((1, 1, TOP_K * tt), lambda i: (i, 0, 0), memory_space=pltpu.SMEM),
                  pl.BlockSpec((1, 1, TOP_K * tt), lambda i: (jnp.minimum(i + 1, nt - 1), 0, 0),
                               memory_space=pltpu.SMEM),
                  pl.BlockSpec(memory_space=pl.ANY),
                  pl.BlockSpec((tt, d), lambda i: (i, 0)),
                  pl.BlockSpec((tt, d), lambda i: (i, 0)),
                  pl.BlockSpec((tt, TOP_K), lambda i: (i, 0)),
                  pl.BlockSpec((1, 6, d), lambda i: (mod_row0 + i // tps, 0, 0))],
        out_specs=pl.BlockSpec((tt, d), lambda i: (i, 0)),
        scratch_shapes=[pltpu.VMEM((2, TOP_K, tt, d), F32), pltpu.SemaphoreType.DMA((2,))],
        compiler_params=_cparams(1),
        name="moe_combine",
    )(slots, slots, yb, x, mixed, wts, mods)


def _slot_blocks(slots, tt):
    k, n = slots.shape
    return slots.reshape(k, n // tt, tt).transpose(1, 0, 2).reshape(n // tt, 1, k * tt)


def _moe(xs, mixed, mods, mod_rows, seq_lens, gain, w_router, b_router, w_gate, w_up, w_down, layer,
         spare=None):
    d = xs[0].shape[1]
    w_router_t = w_router.T
    routed = [_router(x, a, mods, r0, sl, gain, w_router_t, b_router)
              for x, a, r0, sl in zip(xs, mixed, mod_rows, seq_lens)]
    flat_e = jnp.concatenate([eid.reshape(-1) for _, eid, _ in routed])
    n_assign = flat_e.shape[0]
    onehot = (flat_e[:, None] == jnp.arange(N_EXPERTS, dtype=jnp.int32)[None, :]).astype(jnp.int32)
    csum = jnp.cumsum(onehot, axis=0)
    counts = csum[-1]
    padded = (counts + MOE_ROWS - 1) // MOE_ROWS * MOE_ROWS
    pend = jnp.cumsum(padded)
    slot = jnp.sum(onehot * (csum - 1 + (pend - padded)[None, :]), axis=1)
    n_blocks = -(-n_assign // MOE_ROWS) + N_EXPERTS
    if spare is not None and spare.shape == (n_blocks * MOE_ROWS, d):
        buf = spare
    else:
        buf = jnp.zeros((n_blocks * MOE_ROWS, d), F32)
    block_start = jnp.arange(n_blocks, dtype=jnp.int32) * MOE_ROWS
    block_expert = jnp.minimum(jnp.sum((pend[None, :] <= block_start[:, None]).astype(jnp.int32), axis=1),
                               N_EXPERTS - 1)
    n_used = (pend[-1:] // MOE_ROWS).astype(jnp.int32)
    slot_blocks, off = [], 0
    for h2, _, _ in routed:
        n = h2.shape[0]
        sb = _slot_blocks(slot[off:off + TOP_K * n].reshape(TOP_K, n), min(MOE_TT, n))
        off += TOP_K * n
        slot_blocks.append(sb)
        buf = _dispatch(h2, sb, buf)
    yb = _expert_ffn(buf, block_expert, n_used, w_gate, w_up, w_down, layer)
    outs = [_combine(x, a, yb, sb, wts.T, mods, r0, sl)
            for x, a, (_, _, wts), sb, r0, sl in zip(xs, mixed, routed, slot_blocks, mod_rows, seq_lens)]
    return outs, buf


def _rope_tables(n_tok):
    t = jnp.arange(n_tok)
    row = (t // GRID_W).astype(F32)
    col = (t % GRID_W).astype(F32)
    quarter = HEAD_DIM // 4
    inv = ROPE_THETA ** (-jnp.arange(quarter, dtype=F32) / quarter)
    ar = row[:, None] * inv
    ac = col[:, None] * inv
    ang = jnp.concatenate([ar, ar, ac, ac], axis=-1)
    cos, sin = jnp.cos(ang), jnp.sin(ang)
    first = (jnp.arange(HEAD_DIM) % (2 * quarter)) < quarter
    return cos, jnp.where(first, -sin, 0.0), jnp.where(first, 0.0, sin)


def kernel(x_prompt, x_sample, cache_nat_k, cache_nat_v, cache_gqa_k, cache_gqa_v, state_ret_fwd,
           state_ret_bwd, c, c_ctx, w_mod, b_mod, norm1, norm2, w_in, conv_w, nat_qn, nat_kn, nat_rpb,
           gqa_qn, gqa_kn, ret_decay_fwd, ret_decay_bwd, ret_gn, w_branch, w_out, w_router, b_router,
           w_exp_gate, w_exp_up, w_exp_down):
    batch, seq, d = x_prompt.shape
    dec_batch, dec_seq, _ = x_sample.shape
    depth = w_mod.shape[0]
    n_heads = nat_rpb.shape[1]
    n_kv = cache_gqa_k.shape[2]
    hd = HEAD_DIM

    xp = x_prompt.reshape(batch * seq, d)
    xs = x_sample.reshape(dec_batch * dec_seq, d)
    cvecs = jnp.zeros((SUBLANES, d), F32).at[0].set(c_ctx).at[1:1 + dec_batch].set(c)
    mods_all = _modulation(cvecs, w_mod, b_mod).reshape(depth, SUBLANES, 6, d)
    rope_tabs = _rope_tables(dec_seq)
    zero_state = jnp.zeros((1, depth, 1, hd, hd), F32)

    w_in_b, wb, wo = w_in.astype(BF16), w_branch.astype(BF16), w_out.astype(BF16)
    wg, wu, wd = w_exp_gate.astype(BF16), w_exp_up.astype(BF16), w_exp_down.astype(BF16)
    gate_w = w_in.shape[2] - PROJ_W

    caches = [[] for _ in range(6)]
    moe_buf = None
    for l in range(depth):
        mods = mods_all[l]
        lg = jnp.stack([jax.nn.log_sigmoid(ret_decay_fwd[l].astype(F32)),
                        jax.nn.log_sigmoid(ret_decay_bwd[l].astype(F32))], axis=1)
        lg = jnp.broadcast_to(lg[:, :, None], (lg.shape[0], 2, hd))

        pc = _norm_project(xp, mods, 0, 0, norm1[l], w_in_b, l, 0, PROJ_W, gate=False, name="in_proj_ctx")
        gc = _norm_project(xp, mods, 0, 0, norm1[l], w_in_b, l, PROJ_W, gate_w, gate=True, name="gate_proj_ctx")
        conv_c = _short_conv(pc, conv_w[l], seq)
        nat_c, nk, nv = _attention(pc, seq, n_heads, 1, COL_NQ, COL_NK, COL_NV, nat_qn[l], nat_kn[l],
                                   emit_kv=True, name="nat_ctx")
        gqa_c, gk, gv = _attention(pc, seq, n_kv, GQA_GROUP, COL_GQ, COL_GK, COL_GV, gqa_qn[l], gqa_kn[l],
                                   emit_kv=True, name="gqa_ctx")
        ret_c, s_f, s_b = _retention(pc, seq, lg, ret_gn[l], zero_state, zero_state, l)
        for lst, val in zip(caches, (nk, nv, gk, gv, s_f, s_b)):
            lst.append(val)
        mix_c = _merge((conv_c, nat_c, gqa_c, ret_c), gc, wb, wo, l)

        pl_ = _norm_project(xs, mods, 1, dec_seq, norm1[l], w_in_b, l, 0, PROJ_W, gate=False, name="in_proj_lat")
        gl = _norm_project(xs, mods, 1, dec_seq, norm1[l], w_in_b, l, PROJ_W, gate_w, gate=True,
                           name="gate_proj_lat")
        conv_l = _short_conv(pl_, conv_w[l], dec_seq)
        nat_l = _natten(pl_, dec_seq, nat_qn[l], nat_kn[l], nat_rpb[l], cache_nat_k, cache_nat_v, l)
        gqa_l = _attention(pl_, dec_seq, n_kv, GQA_GROUP, COL_GQ, COL_GK, COL_GV, gqa_qn[l], gqa_kn[l],
                           rope_tabs=rope_tabs, cache=(cache_gqa_k, cache_gqa_v, l), name="gqa_lat")
        ret_l, _, _ = _retention(pl_, dec_seq, lg, ret_gn[l], state_ret_fwd, state_ret_bwd, l)
        mix_l = _merge((conv_l, nat_l, gqa_l, ret_l), gl, wb, wo, l)

        (xp, xs), moe_buf = _moe([xp, xs], [mix_c, mix_l], mods, [0, 1], [0, dec_seq], norm2[l], w_router,
                                 b_router, wg, wu, wd, l, spare=moe_buf)

    outs = [jnp.stack(v, axis=1) for v in caches]
    return (xp.reshape(batch, seq, d), xs.reshape(dec_batch, dec_seq, d), *outs)
```

```python
import functools

import numpy as np
import jax
import jax.numpy as jnp
from jax import lax
from jax.experimental import pallas as pl
from jax.experimental.pallas import tpu as pltpu

F32 = jnp.float32
BF16 = jnp.bfloat16

GRID_W = 64
HEAD_DIM = 128
BRANCH_W = 512
N_BRANCH = 4
CONV_K = 3
NAT_KR = 8
NAT_KC = 16
GQA_GROUP = 2
RET_CHUNK = 128
ROPE_THETA = 10000.0
N_EXPERTS = 16
N_GROUPS = 4
TOP_K = 2
EPS = 1e-6
MASKED = -1e30
LOG2_E = 1.4426950408889634

COL_U, COL_BG, COL_CG = 0, 512, 1024
COL_NQ, COL_NK, COL_NV = 1536, 2048, 2560
COL_GQ, COL_GK, COL_GV = 3072, 3584, 3840
COL_RQ, COL_RK, COL_RV, COL_RG = 4096, 4608, 5120, 5632
PROJ_W = 6144

V7X_VMEM_LIMIT_BYTES = 60 * 1024 * 1024
SUBLANES = 8
LANES = 128

PROJ_TM, PROJ_TN = 1024, 2048
NORM_ROWS = 256
MERGE_TM, MERGE_TK = 1024, 512
ROUTER_TM = 512
MOE_ROWS = 256
MOE_TT = 256
CONV_TT = 1024
ATT_TQ, ATT_TK = 1024, 512
MOD_TN = 1024
NAT_ROWS = 32
RET_UNROLL = 16
DMA_UNROLL = 8


def _cparams(n_axes):
    return pltpu.CompilerParams(dimension_semantics=("arbitrary",) * n_axes,
                                vmem_limit_bytes=V7X_VMEM_LIMIT_BYTES)


def _sigmoid(x):
    return 1.0 / (1.0 + jnp.exp(-x))


def _dot(a, b):
    return jnp.dot(a, b, preferred_element_type=F32)


def _dot_nt(a, b):
    return lax.dot_general(a, b, (((1,), (1,)), ((), ())), preferred_element_type=F32)


def _rms(x, gain):
    return x * lax.rsqrt(jnp.mean(x * x, axis=-1, keepdims=True) + EPS) * gain


def _rows(i, n):
    return pl.ds(pl.multiple_of(i * n, n), n)


def _mod_kernel(c_ref, w_ref, b_ref, o_ref):
    c = c_ref[...]
    a = (c * _sigmoid(c)).astype(BF16)
    o_ref[0] = _dot(a, w_ref[0].astype(BF16)) + b_ref[0]


def _modulation(cvecs, w_mod, b_mod):
    depth, d, n = w_mod.shape
    tn = min(MOD_TN, n)
    return pl.pallas_call(
        _mod_kernel,
        out_shape=jax.ShapeDtypeStruct((depth, cvecs.shape[0], n), F32),
        grid=(depth, n // tn),
        in_specs=[pl.BlockSpec(cvecs.shape, lambda l, j: (0, 0)),
                  pl.BlockSpec((1, d, tn), lambda l, j: (l, 0, j)),
                  pl.BlockSpec((1, 1, tn), lambda l, j: (l, 0, j))],
        out_specs=pl.BlockSpec((1, cvecs.shape[0], tn), lambda l, j: (l, 0, j)),
        compiler_params=_cparams(2),
        name="modulation",
    )(cvecs, w_mod, b_mod.reshape(depth, 1, n))


def _normproj_kernel(x_ref, mod_ref, g_ref, w_ref, o_ref, h_scr, *, gate):
    @pl.when(pl.program_id(1) == 0)
    def _():
        gain = g_ref[...] * (1.0 + mod_ref[0, 1:2, :])
        shift = mod_ref[0, 0:1, :]

        def chunk(i, c):
            rows = _rows(i, NORM_ROWS)
            x = x_ref[rows, :]
            y = x * lax.rsqrt(jnp.mean(x * x, axis=-1, keepdims=True) + EPS)
            h_scr[rows, :] = (y * gain + shift).astype(BF16)
            return c

        lax.fori_loop(0, x_ref.shape[0] // NORM_ROWS, chunk, 0)

    acc = _dot(h_scr[...], w_ref[0])
    o_ref[...] = _sigmoid(acc) if gate else acc


def _norm_project(x, mods, mod_row0, tiles_per_seq, gain, w_all, layer, col0, nout, *, gate, name):
    n, d = x.shape
    tm, tn = min(PROJ_TM, n), min(PROJ_TN, nout)
    assert col0 % tn == 0 and nout % tn == 0
    tps = max(tiles_per_seq // tm, 1) if tiles_per_seq else n // tm
    return pl.pallas_call(
        functools.partial(_normproj_kernel, gate=gate),
        out_shape=jax.ShapeDtypeStruct((n, nout), F32),
        grid=(n // tm, nout // tn),
        in_specs=[pl.BlockSpec((tm, d), lambda i, j: (i, 0)),
                  pl.BlockSpec((1, 6, d), lambda i, j: (mod_row0 + i // tps, 0, 0)),
                  pl.BlockSpec((1, d), lambda i, j: (0, 0)),
                  pl.BlockSpec((1, d, tn), lambda i, j: (layer, 0, col0 // tn + j))],
        out_specs=pl.BlockSpec((tm, tn), lambda i, j: (i, j)),
        scratch_shapes=[pltpu.VMEM((tm, d), BF16)],
        compiler_params=_cparams(2),
        name=name,
    )(x, mods, gain.reshape(1, d), w_all)


def _conv_kernel(u_ref, bg_ref, cg_ref, up_ref, cp_ref, un_ref, cn_ref, w_ref, o_ref, *, tt, nt):
    t = pl.program_id(1)
    z = cg_ref[...] * u_ref[...]
    last = SUBLANES - 1
    z_before = jnp.where(t > 0, cp_ref[last:last + 1, :] * up_ref[last:last + 1, :], 0.0)
    z_after = jnp.where(t < nt - 1, cn_ref[0:1, :] * un_ref[0:1, :], 0.0)
    ri = lax.broadcasted_iota(jnp.int32, z.shape, 0)
    z_prev = jnp.where(ri == 0, z_before, pltpu.roll(z, 1, 0))
    z_next = jnp.where(ri == tt - 1, z_after, pltpu.roll(z, tt - 1, 0))
    w = w_ref[...]
    o_ref[...] = (bg_ref[...] * (w[0:1, :] * z_prev + w[1:2, :] * z + w[2:3, :] * z_next)).astype(BF16)


def _short_conv(proj, conv_w, seq_len):
    n = proj.shape[0]
    cw = conv_w.shape[1]
    nseq = n // seq_len
    tt = min(CONV_TT, seq_len)
    nt = seq_len // tt
    cu, cb, cc = COL_U // cw, COL_BG // cw, COL_CG // cw
    nblk8 = n // SUBLANES

    def main(col):
        return pl.BlockSpec((tt, cw), lambda b, t: (b * nt + t, col))

    def before(col):
        return pl.BlockSpec((SUBLANES, cw),
                            lambda b, t: (jnp.maximum((b * nt + t) * (tt // SUBLANES) - 1, 0), col))

    def after(col):
        return pl.BlockSpec((SUBLANES, cw),
                            lambda b, t: (jnp.minimum((b * nt + t + 1) * (tt // SUBLANES), nblk8 - 1), col))

    return pl.pallas_call(
        functools.partial(_conv_kernel, tt=tt, nt=nt),
        out_shape=jax.ShapeDtypeStruct((n, cw), BF16),
        grid=(nseq, nt),
        in_specs=[main(cu), main(cb), main(cc), before(cu), before(cc), after(cu), after(cc),
                  pl.BlockSpec(conv_w.shape, lambda b, t: (0, 0))],
        out_specs=pl.BlockSpec((tt, cw), lambda b, t: (b * nt + t, 0)),
        compiler_params=_cparams(2),
        name="short_conv",
    )(proj, proj, proj, proj, proj, proj, proj, conv_w)


def _attn_kernel(*refs, group, seq, tq, tk, rope, cache_len, emit_kv):
    it = iter(refs)
    q_ref, k_ref, v_ref, qn_ref, kn_ref = (next(it) for _ in range(5))
    if rope:
        cos_ref, sin_lo_ref, sin_hi_ref = (next(it) for _ in range(3))
    if cache_len:
        kc_ref, vc_ref = next(it), next(it)
    o_ref = next(it)
    if emit_kv:
        ko_ref, vo_ref = next(it), next(it)
    kb, vb, m_scr, l_scr, acc, q_scr = (next(it) for _ in range(6))
    scale = HEAD_DIM ** -0.5 * LOG2_E
    hd = HEAD_DIM

    def prep(x, gain_ref, rows):
        y = _rms(x, gain_ref[...])
        if rope:
            y = (y * cos_ref[rows, :] + pltpu.roll(y, hd - hd // 4, 1) * sin_lo_ref[rows, :]
                 + pltpu.roll(y, hd // 4, 1) * sin_hi_ref[rows, :])
        return y

    def key_tile(kt, c):
        rows = _rows(kt, tk)
        kn = prep(k_ref[rows, :], kn_ref, rows)
        v = v_ref[rows, :]
        kb[rows, :] = kn.astype(BF16)
        vb[rows, :] = v.astype(BF16)
        if emit_kv:
            ko_ref[0, 0, rows, :] = kn
            vo_ref[0, 0, rows, :] = v
        return c

    lax.fori_loop(0, seq // tk, key_tile, 0)

    def softmax_step(s, vblk):
        m_prev = m_scr[...]
        m_new = jnp.maximum(m_prev, jnp.max(s, axis=-1, keepdims=True))
        alpha = jnp.exp2(m_prev - m_new)
        p = jnp.exp2(s - jnp.concatenate([m_new] * (s.shape[1] // hd), axis=1))
        l_scr[...] = alpha * l_scr[...] + jnp.sum(p, axis=-1, keepdims=True)
        acc[...] = alpha * acc[...] + _dot(p.astype(BF16), vblk)
        m_scr[...] = m_new

    def query_tile(qt, c):
        rows = _rows(qt, tq)
        for g in range(group):
            q_scr[g * tq:(g + 1) * tq, :] = (
                prep(q_ref[rows, g * hd:(g + 1) * hd], qn_ref, rows) * scale).astype(BF16)
        m_scr[...] = jnp.full(m_scr.shape, MASKED, F32)
        l_scr[...] = jnp.zeros(l_scr.shape, F32)
        acc[...] = jnp.zeros(acc.shape, F32)
        n_kv = seq // tk

        def scores(kt):
            return _dot_nt(q_scr[...], kb[_rows(kt, tk), :])

        def kv_tile(kt, c2):
            softmax_step(scores(kt), vb[_rows(kt, tk), :])
            return c2

        lax.fori_loop(0, n_kv, kv_tile, 0)
        if cache_len:
            softmax_step(_dot_nt(q_scr[...], kc_ref[0, 0, 0].astype(BF16)), vc_ref[0, 0, 0].astype(BF16))
        o = acc[...] / l_scr[...]
        for g in range(group):
            o_ref[rows, g * hd:(g + 1) * hd] = o[g * tq:(g + 1) * tq].astype(BF16)
        return c

    lax.fori_loop(0, seq // tq, query_tile, 0)


def _attention(proj, seq_len, n_kv, group, q_col, k_col, v_col, q_gain, k_gain, *,
               rope_tabs=None, cache=None, emit_kv=False, name):
    n = proj.shape[0]
    nseq = n // seq_len
    hd = HEAD_DIM
    tq, tk = min(ATT_TQ, seq_len), min(ATT_TK, seq_len)
    gw = group * hd
    in_specs = [pl.BlockSpec((seq_len, gw), lambda b, h: (b, q_col // gw + h)),
                pl.BlockSpec((seq_len, hd), lambda b, h: (b, k_col // hd + h)),
                pl.BlockSpec((seq_len, hd), lambda b, h: (b, v_col // hd + h)),
                pl.BlockSpec((1, hd), lambda b, h: (0, 0)),
                pl.BlockSpec((1, hd), lambda b, h: (0, 0))]
    args = [proj, proj, proj, q_gain.reshape(1, hd), k_gain.reshape(1, hd)]
    if rope_tabs is not None:
        in_specs += [pl.BlockSpec((seq_len, hd), lambda b, h: (0, 0))] * 3
        args += list(rope_tabs)
    cache_len = 0
    if cache is not None:
        kc, vc, layer = cache
        cache_len = kc.shape[3]
        in_specs += [pl.BlockSpec((1, 1, 1, cache_len, hd), lambda b, h: (b, layer, h, 0, 0))] * 2
        args += [kc, vc]
    out_shape = [jax.ShapeDtypeStruct((n, n_kv * gw), BF16)]
    out_specs = [pl.BlockSpec((seq_len, gw), lambda b, h: (b, h))]
    if emit_kv:
        out_shape += [jax.ShapeDtypeStruct((nseq, n_kv, seq_len, hd), F32)] * 2
        out_specs += [pl.BlockSpec((1, 1, seq_len, hd), lambda b, h: (b, h, 0, 0))] * 2
    m = group * tq
    res = pl.pallas_call(
        functools.partial(_attn_kernel, group=group, seq=seq_len, tq=tq, tk=tk,
                          rope=rope_tabs is not None, cache_len=cache_len, emit_kv=emit_kv),
        out_shape=out_shape,
        grid=(nseq, n_kv),
        in_specs=in_specs,
        out_specs=out_specs,
        scratch_shapes=[pltpu.VMEM((seq_len, hd), BF16), pltpu.VMEM((seq_len, hd), BF16),
                        pltpu.VMEM((m, hd), F32), pltpu.VMEM((m, hd), F32), pltpu.VMEM((m, hd), F32),
                        pltpu.VMEM((m, hd), BF16)],
        compiler_params=_cparams(2),
        name=name,
    )(*args)
    return res if emit_kv else res[0]


def _natten_kernel(q_ref, k_ref, v_ref, qn_ref, kn_ref, kc_ref, vc_ref, bias_ref, o_ref, kb, vb, *,
                   seq, width, kr, chunk):
    n_rows = seq // width
    scale = HEAD_DIM ** -0.5 * LOG2_E

    def key_chunk(i, c):
        rows = _rows(i, chunk)
        kb[rows, :] = _rms(k_ref[rows, :], kn_ref[...]).astype(BF16)
        vb[rows, :] = v_ref[rows, :].astype(BF16)
        return c

    lax.fori_loop(0, seq // chunk, key_chunk, 0)

    rg = NAT_ROWS if n_rows % NAT_ROWS == 0 else 1

    def row_group(gi, c):
        qrows = _rows(gi, rg * width)
        q = (_rms(q_ref[qrows, :], qn_ref[...]) * scale).astype(BF16)
        s_ctx = _dot_nt(q, kc_ref[0, 0, 0].astype(BF16))
        krows, s_loc = [], []
        for j in range(rg):
            r = gi * rg + j
            r0 = jnp.clip(r - kr // 2, 0, n_rows - kr)
            krows.append(pl.ds(pl.multiple_of(r0 * width, width), kr * width))
            s_loc.append(_dot_nt(q[j * width:(j + 1) * width], kb[krows[j], :]) + bias_ref[0, r - r0])
        s_loc = jnp.concatenate(s_loc, axis=0) if rg > 1 else s_loc[0]
        m = jnp.maximum(jnp.max(s_loc, axis=-1, keepdims=True), jnp.max(s_ctx, axis=-1, keepdims=True))
        p_loc = jnp.exp2(s_loc - m)
        p_ctx = jnp.exp2(s_ctx - m)
        denom = jnp.sum(p_loc, axis=-1, keepdims=True) + jnp.sum(p_ctx, axis=-1, keepdims=True)
        p_loc = p_loc.astype(BF16)
        o_loc = [_dot(p_loc[j * width:(j + 1) * width], vb[krows[j], :]) for j in range(rg)]
        o_loc = jnp.concatenate(o_loc, axis=0) if rg > 1 else o_loc[0]
        o = o_loc + _dot(p_ctx.astype(BF16), vc_ref[0, 0, 0].astype(BF16))
        o_ref[qrows, :] = (o / denom).astype(BF16)
        return c

    lax.fori_loop(0, n_rows // rg, row_group, 0)


def _natten_bias(rpb, width, kr):
    kc = NAT_KC
    cols = np.arange(width)
    c0 = np.clip(cols - kc // 2, 0, width - kc)
    j = np.arange(width)
    in_win = (j[None, :] >= c0[:, None]) & (j[None, :] < c0[:, None] + kc)
    ci = np.clip(j[None, :] - cols[:, None] + kc - 1, 0, 2 * kc - 2)
    off = np.arange(kr)
    ri = np.arange(kr)[None, :] - off[:, None] + NAT_KR - 1
    pick_r = np.zeros((kr * kr, 2 * NAT_KR - 1), np.float32)
    pick_r[np.arange(kr * kr), ri.reshape(-1)] = 1.0
    pick_c = np.zeros((2 * kc - 1, width * width), np.float32)
    pick_c[ci.reshape(-1), np.arange(width * width)] = 1.0
    tab = jnp.einsum('ar,hrs,sb->hab', pick_r, rpb.astype(F32), pick_c, precision=lax.Precision.HIGHEST)
    tab = tab.reshape(rpb.shape[0], kr, kr, width, width).transpose(0, 1, 3, 2, 4)
    tab = jnp.where(in_win[None, None, :, None, :], tab * LOG2_E, MASKED)
    return tab.reshape(rpb.shape[0], kr, width, kr * width)


def _natten(proj, seq_len, q_gain, k_gain, rpb, kc, vc, layer):
    n = proj.shape[0]
    nseq = n // seq_len
    hd = HEAD_DIM
    n_heads = rpb.shape[0]
    width = GRID_W
    kr = min(NAT_KR, seq_len // width)
    bias = _natten_bias(rpb, width, kr)
    cache_len = kc.shape[3]
    chunk = min(512, seq_len)
    col = lambda c0: pl.BlockSpec((seq_len, hd), lambda b, h: (b, c0 // hd + h))
    return pl.pallas_call(
        functools.partial(_natten_kernel, seq=seq_len, width=width, kr=kr, chunk=chunk),
        out_shape=jax.ShapeDtypeStruct((n, n_heads * hd), BF16),
        grid=(nseq, n_heads),
        in_specs=[col(COL_NQ), col(COL_NK), col(COL_NV),
                  pl.BlockSpec((1, hd), lambda b, h: (0, 0)),
                  pl.BlockSpec((1, hd), lambda b, h: (0, 0)),
                  pl.BlockSpec((1, 1, 1, cache_len, hd), lambda b, h: (b, layer, h, 0, 0)),
                  pl.BlockSpec((1, 1, 1, cache_len, hd), lambda b, h: (b, layer, h, 0, 0)),
                  pl.BlockSpec((1, kr, width, kr * width), lambda b, h: (h, 0, 0, 0))],
        out_specs=pl.BlockSpec((seq_len, hd), lambda b, h: (b, h)),
        scratch_shapes=[pltpu.VMEM((seq_len, hd), BF16), pltpu.VMEM((seq_len, hd), BF16)],
        compiler_params=_cparams(2),
        name="natten_latent",
    )(proj, proj, proj, q_gain.reshape(1, hd), k_gain.reshape(1, hd), kc, vc, bias)


def _retention_kernel(q_ref, k_ref, v_ref, rg_ref, lg_ref, gn_ref, sf0_ref, sb0_ref,
                      y_ref, sf_ref, sb_ref, o_scr, kv_scr, sp_scr, *, seq, unroll):
    c = RET_CHUNK
    nc = seq // c
    hd = HEAD_DIM
    u = unroll
    lgf = lg_ref[0, 0:1, :]
    lgb = lg_ref[0, 1:2, :]
    ii = lax.broadcasted_iota(jnp.int32, (c, c), 0)
    jj = lax.broadcasted_iota(jnp.int32, (c, c), 1)
    rel = (ii - jj).astype(F32)
    pos = lax.broadcasted_iota(jnp.int32, (c, hd), 0).astype(F32)
    decay = (jnp.where(rel >= 0, jnp.exp(lgf * jnp.maximum(rel, 0.0)), 0.0)
             + jnp.where(rel <= 0, jnp.exp(lgb * jnp.maximum(-rel, 0.0)), 0.0))
    qd = jnp.concatenate([jnp.exp(lgf * (pos + 1.0)), jnp.exp(lgb * (c - pos))], axis=1)
    kd = jnp.concatenate([jnp.exp(lgf * (c - 1.0 - pos)), jnp.exp(lgb * pos)], axis=1)
    cd_f = jnp.exp(lgf * c)
    cd_b = jnp.exp(lgb * c)
    kscale = HEAD_DIM ** -0.5

    def within_chunks(gi, carry):
        rows = [_rows(gi * u + j, c) for j in range(u)]
        k = [k_ref[r, :] * kscale for r in rows]
        v = [v_ref[r, :].astype(BF16) for r in rows]
        qk = [_dot_nt(q_ref[rows[j], :].astype(BF16), k[j].astype(BF16)) for j in range(u)]
        att = [(s * decay).astype(BF16) for s in qk]
        for j in range(u):
            o_scr[rows[j], :] = _dot(att[j], v[j])
        for j in range(u):
            kk = jnp.concatenate([k[j], k[j]], axis=1) * kd
            kv_scr[gi * u + j] = _dot(kk.T.astype(BF16), v[j])
        return carry

    lax.fori_loop(0, nc // u, within_chunks, 0)

    def recurrence(t, carry):
        sf, sb = carry
        tb = nc - 1 - t
        sp_scr[t, 0:hd, :] = sf.astype(BF16)
        sp_scr[tb, hd:2 * hd, :] = sb.astype(BF16)
        return sf * cd_f + kv_scr[t, 0:hd, :], sb * cd_b + kv_scr[tb, hd:2 * hd, :]

    sf, sb = lax.fori_loop(0, nc, recurrence, (sf0_ref[0, 0, 0], sb0_ref[0, 0, 0]))
    sf_ref[0, 0] = sf
    sb_ref[0, 0] = sb

    def across_chunks(gi, carry):
        rows = [_rows(gi * u + j, c) for j in range(u)]
        o = []
        for j in range(u):
            q = q_ref[rows[j], :]
            qq = (jnp.concatenate([q, q], axis=1) * qd).astype(BF16)
            o.append(o_scr[rows[j], :] + _dot(qq, sp_scr[gi * u + j]))
        for j in range(u):
            on = o[j] * lax.rsqrt(jnp.mean(o[j] * o[j], axis=-1, keepdims=True) + EPS)
            g = rg_ref[rows[j], :]
            y_ref[rows[j], :] = ((g * _sigmoid(g)) * (on * gn_ref[...])).astype(BF16)
        return carry

    lax.fori_loop(0, nc // u, across_chunks, 0)


def _retention(proj, seq_len, log_gamma, ret_gn, s_fwd, s_bwd, layer):
    n = proj.shape[0]
    nseq = n // seq_len
    hd = HEAD_DIM
    n_heads = log_gamma.shape[0]
    nc = seq_len // RET_CHUNK
    shared = s_fwd.shape[0] == 1
    col = lambda c0: pl.BlockSpec((seq_len, hd), lambda b, h: (b, c0 // hd + h))
    state = pl.BlockSpec((1, 1, 1, hd, hd), lambda b, h: (0 if shared else b, layer, 0 if shared else h, 0, 0))
    return pl.pallas_call(
        functools.partial(_retention_kernel, seq=seq_len, unroll=next(u for u in (RET_UNROLL, 2, 1) if nc % u == 0)),
        out_shape=[jax.ShapeDtypeStruct((n, n_heads * hd), BF16),
                   jax.ShapeDtypeStruct((nseq, n_heads, hd, hd), F32),
                   jax.ShapeDtypeStruct((nseq, n_heads, hd, hd), F32)],
        grid=(nseq, n_heads),
        in_specs=[col(COL_RQ), col(COL_RK), col(COL_RV), col(COL_RG),
                  pl.BlockSpec((1, 2, hd), lambda b, h: (h, 0, 0)),
                  pl.BlockSpec((1, hd), lambda b, h: (0, h)),
                  state, state],
        out_specs=[pl.BlockSpec((seq_len, hd), lambda b, h: (b, h)),
                   pl.BlockSpec((1, 1, hd, hd), lambda b, h: (b, h, 0, 0)),
                   pl.BlockSpec((1, 1, hd, hd), lambda b, h: (b, h, 0, 0))],
        scratch_shapes=[pltpu.VMEM((seq_len, hd), F32), pltpu.VMEM((nc, 2 * hd, hd), F32),
                        pltpu.VMEM((nc, 2 * hd, hd), BF16)],
        compiler_params=_cparams(2),
        name="retention",
    )(proj, proj, proj, proj, log_gamma, ret_gn.reshape(1, n_heads * hd), s_fwd, s_bwd)


def _merge_kernel(y0_ref, y1_ref, y2_ref, y3_ref, g0_ref, g1_ref, g2_ref, g3_ref, wb_ref, wo_ref, o_ref):
    k = pl.program_id(1)
    m = g0_ref[...] * _dot(y0_ref[...], wb_ref[0, 0])
    for i, (y_ref, g_ref) in enumerate(((y1_ref, g1_ref), (y2_ref, g2_ref), (y3_ref, g3_ref)), start=1):
        m = m + g_ref[...] * _dot(y_ref[...], wb_ref[0, i])
    part = _dot(m.astype(BF16), wo_ref[0])

    @pl.when(k == 0)
    def _():
        o_ref[...] = part

    @pl.when(k > 0)
    def _():
        o_ref[...] += part


def _merge(branches, gates, w_branch, w_out, layer):
    n = gates.shape[0]
    _, _, bw, d = w_branch.shape
    tm, tk = min(MERGE_TM, n), min(MERGE_TK, d)
    nk = d // tk
    gate = lambda i: pl.BlockSpec((tm, tk), lambda t, k: (t, i * nk + k))
    return pl.pallas_call(
        _merge_kernel,
        out_shape=jax.ShapeDtypeStruct((n, d), F32),
        grid=(n // tm, nk),
        in_specs=[pl.BlockSpec((tm, bw), lambda t, k: (t, 0))] * 4
                 + [gate(0), gate(1), gate(2), gate(3),
                    pl.BlockSpec((1, N_BRANCH, bw, tk), lambda t, k: (layer, 0, 0, k)),
                    pl.BlockSpec((1, tk, d), lambda t, k: (layer, k, 0))],
        out_specs=pl.BlockSpec((tm, d), lambda t, k: (t, 0)),
        compiler_params=_cparams(2),
        name="merge_out",
    )(*branches, gates, gates, gates, gates, w_branch, w_out)


def _router_kernel(x_ref, a_ref, mod_ref, g_ref, wt_ref, b_ref, h_ref, eid_ref, wts_ref):
    x1 = x_ref[...] + mod_ref[0, 2:3, :] * a_ref[...]
    y = _rms(x1, g_ref[...])
    h = y * (1.0 + mod_ref[0, 4:5, :]) + mod_ref[0, 3:4, :]
    h_ref[...] = h
    h_hi = h.astype(BF16)
    h_lo = (h - h_hi.astype(F32)).astype(BF16)
    w = wt_ref[...]
    w_hi = w.astype(BF16)
    w_lo = (w - w_hi.astype(F32)).astype(BF16)
    logits = _dot_nt(w_hi, h_hi) + (_dot_nt(w_lo, h_hi) + _dot_nt(w_hi, h_lo))
    score = _sigmoid(logits)
    sel = score + b_ref[...]
    epg = N_EXPERTS // N_GROUPS
    s = [sel[e:e + 1, :] for e in range(N_EXPERTS)]
    sc = [score[e:e + 1, :] for e in range(N_EXPERTS)]

    def group_score(vals):
        best = None
        for a in range(len(vals)):
            for b in range(a + 1, len(vals)):
                pair = vals[a] + vals[b]
                best = pair if best is None else jnp.maximum(best, pair)
        return best

    gs = [group_score(s[g * epg:(g + 1) * epg]) for g in range(N_GROUPS)]
    g_best = jnp.zeros(gs[0].shape, jnp.int32)
    best = gs[0]
    for g in range(1, N_GROUPS):
        better = gs[g] > best
        g_best = jnp.where(better, g, g_best)
        best = jnp.where(better, gs[g], best)
    in_sel, in_score = [], []
    for k in range(epg):
        v, w_ = s[k], sc[k]
        for g in range(1, N_GROUPS):
            v = jnp.where(g_best == g, s[g * epg + k], v)
            w_ = jnp.where(g_best == g, sc[g * epg + k], w_)
        in_sel.append(v)
        in_score.append(w_)
    i1 = jnp.zeros(g_best.shape, jnp.int32)
    v1, w1 = in_sel[0], in_score[0]
    for k in range(1, epg):
        better = in_sel[k] > v1
        i1 = jnp.where(better, k, i1)
        v1 = jnp.where(better, in_sel[k], v1)
        w1 = jnp.where(better, in_score[k], w1)
    i2 = jnp.zeros(g_best.shape, jnp.int32)
    v2 = jnp.full(v1.shape, -jnp.inf, F32)
    w2 = jnp.zeros(v1.shape, F32)
    for k in range(epg):
        better = (i1 != k) & (in_sel[k] > v2)
        i2 = jnp.where(better, k, i2)
        v2 = jnp.where(better, in_sel[k], v2)
        w2 = jnp.where(better, in_score[k], w2)
    total = w1 + w2
    eid_ref[...] = jnp.concatenate([g_best * epg + i1, g_best * epg + i2], axis=0)
    wts_ref[...] = jnp.concatenate([w1 / total, w2 / total], axis=0)


def _router(x, mixed, mods, mod_row0, tiles_per_seq, gain, w_router_t, b_router):
    n, d = x.shape
    tm = min(ROUTER_TM, n)
    tps = max(tiles_per_seq // tm, 1) if tiles_per_seq else n // tm
    e = w_router_t.shape[0]
    tile = pl.BlockSpec((tm, d), lambda i: (i, 0))
    return pl.pallas_call(
        _router_kernel,
        out_shape=[jax.ShapeDtypeStruct((n, d), F32),
                   jax.ShapeDtypeStruct((TOP_K, n), jnp.int32),
                   jax.ShapeDtypeStruct((TOP_K, n), F32)],
        grid=(n // tm,),
        in_specs=[tile, tile,
                  pl.BlockSpec((1, 6, d), lambda i: (mod_row0 + i // tps, 0, 0)),
                  pl.BlockSpec((1, d), lambda i: (0, 0)),
                  pl.BlockSpec((e, d), lambda i: (0, 0)),
                  pl.BlockSpec((e, 1), lambda i: (0, 0))],
        out_specs=[tile,
                   pl.BlockSpec((TOP_K, tm), lambda i: (0, i)),
                   pl.BlockSpec((TOP_K, tm), lambda i: (0, i))],
        compiler_params=_cparams(1),
        name="router",
    )(x, mixed, mods, gain.reshape(1, d), w_router_t, b_router.reshape(e, 1))


def _row_copy(src_ref, src_row, dst_ref, dst_row, sem):
    return pltpu.make_async_copy(src_ref.at[pl.ds(src_row, 1)], dst_ref.at[pl.ds(dst_row, 1)], sem)


def _dispatch_kernel(slot_ref, h_ref, buf_in_ref, buf_ref, sem, *, tt):
    del buf_in_ref

    def copies(j):
        return [_row_copy(h_ref, j, buf_ref, slot_ref[0, 0, k * tt + j], sem) for k in range(TOP_K)]

    def start(j, c):
        for k, cp in enumerate(copies(j)):
            cp.start(priority=k % 2)
        return c

    def wait(j, c):
        for cp in copies(j):
            cp.wait()
        return c

    lax.fori_loop(0, tt, start, 0, unroll=DMA_UNROLL)
    lax.fori_loop(0, tt, wait, 0, unroll=DMA_UNROLL)


def _dispatch(h, slots, buf):
    n, d = h.shape
    tt = min(MOE_TT, n)
    return pl.pallas_call(
        functools.partial(_dispatch_kernel, tt=tt),
        out_shape=jax.ShapeDtypeStruct(buf.shape, buf.dtype),
        grid=(n // tt,),
        in_specs=[pl.BlockSpec((1, 1, TOP_K * tt), lambda i: (i, 0, 0), memory_space=pltpu.SMEM),
                  pl.BlockSpec((tt, d), lambda i: (i, 0)),
                  pl.BlockSpec(memory_space=pl.ANY)],
        out_specs=pl.BlockSpec(memory_space=pl.ANY),
        scratch_shapes=[pltpu.SemaphoreType.DMA],
        input_output_aliases={2: 0},
        compiler_params=_cparams(1),
        name="moe_dispatch",
    )(slots, h, buf)


def _ffn_kernel(be_ref, nu_ref, x_ref, wg_ref, wu_ref, wd_ref, o_ref):
    del be_ref
    live = pl.program_id(0) < nu_ref[0]

    @pl.when(live)
    def _():
        x = x_ref[...].astype(BF16)
        a = _dot(x, wg_ref[0, 0])
        b = _dot(x, wu_ref[0, 0])
        o_ref[...] = _dot(((a * _sigmoid(a)) * b).astype(BF16), wd_ref[0, 0])

    @pl.when(jnp.logical_not(live))
    def _():
        o_ref[...] = jnp.zeros(o_ref.shape, F32)


def _expert_ffn(buf, block_expert, n_used, w_gate, w_up, w_down, layer):
    rows, d = buf.shape
    de = w_gate.shape[3]
    rb = MOE_ROWS
    grid_spec = pltpu.PrefetchScalarGridSpec(
        num_scalar_prefetch=2,
        grid=(rows // rb,),
        in_specs=[pl.BlockSpec((rb, d), lambda i, be, nu: (i, 0)),
                  pl.BlockSpec((1, 1, d, de), lambda i, be, nu: (layer, be[i], 0, 0)),
                  pl.BlockSpec((1, 1, d, de), lambda i, be, nu: (layer, be[i], 0, 0)),
                  pl.BlockSpec((1, 1, de, d), lambda i, be, nu: (layer, be[i], 0, 0))],
        out_specs=pl.BlockSpec((rb, d), lambda i, be, nu: (i, 0)))
    return pl.pallas_call(
        _ffn_kernel,
        out_shape=jax.ShapeDtypeStruct((rows, d), F32),
        grid_spec=grid_spec,
        compiler_params=_cparams(1),
        name="expert_ffn",
    )(block_expert, n_used, buf, w_gate, w_up, w_down)


def _combine_kernel(slot_ref, next_slot_ref, yb_ref, x_ref, a_ref, w_ref, mod_ref, o_ref, gath, sems, *, tt):
    i = pl.program_id(0)
    cur = i % 2

    def copies(s_ref, half, j):
        return [_row_copy(yb_ref, s_ref[0, 0, k * tt + j], gath.at[half, k], j, sems.at[half])
                for k in range(TOP_K)]

    def fetch(s_ref, half):
        def start(j, c):
            for k, cp in enumerate(copies(s_ref, half, j)):
                cp.start(priority=k % 2)
            return c
        lax.fori_loop(0, tt, start, 0, unroll=DMA_UNROLL)

    @pl.when(i == 0)
    def _():
        fetch(slot_ref, 0)

    @pl.when(i + 1 < pl.num_programs(0))
    def _():
        fetch(next_slot_ref, 1 - cur)

    def wait(j, c):
        for cp in copies(slot_ref, cur, j):
            cp.wait()
        return c

    lax.fori_loop(0, tt, wait, 0, unroll=DMA_UNROLL)
    w = w_ref[...]
    y = w[:, 0:1] * gath[cur, 0] + w[:, 1:2] * gath[cur, 1]
    x1 = x_ref[...] + mod_ref[0, 2:3, :] * a_ref[...]
    o_ref[...] = x1 + mod_ref[0, 5:6, :] * y


def _combine(x, mixed, yb, slots, wts, mods, mod_row0, tiles_per_seq):
    n, d = x.shape
    tt = min(MOE_TT, n)
    tps = max(tiles_per_seq // tt, 1) if tiles_per_seq else n // tt
    nt = n // tt
    return pl.pallas_call(
        functools.partial(_combine_kernel, tt=tt),
        out_shape=jax.ShapeDtypeStruct((n, d), F32),
        grid=(nt,),
        in_specs=[pl.BlockSpec((1, 1, TOP_K * tt), lambda i: (i, 0, 0), memory_space=pltpu.SMEM),
                  pl.BlockSpec((1, 1, TOP_K * tt), lambda i: (jnp.minimum(i + 1, nt - 1), 0, 0),
                               memory_space=pltpu.SMEM),
                  pl.BlockSpec(memory_space=pl.ANY),
                  pl.BlockSpec((tt, d), lambda i: (i, 0)),
                  pl.BlockSpec((tt, d), lambda i: (i, 0)),
                  pl.BlockSpec((tt, TOP_K), lambda i: (i, 0)),
                  pl.BlockSpec((1, 6, d), lambda i: (mod_row0 + i // tps, 0, 0))],
        out_specs=pl.BlockSpec((tt, d), lambda i: (i, 0)),
        scratch_shapes=[pltpu.VMEM((2, TOP_K, tt, d), F32), pltpu.SemaphoreType.DMA((2,))],
        compiler_params=_cparams(1),
        name="moe_combine",
    )(slots, slots, yb, x, mixed, wts, mods)


def _slot_blocks(slots, tt):
    k, n = slots.shape
    return slots.reshape(k, n // tt, tt).transpose(1, 0, 2).reshape(n // tt, 1, k * tt)


def _moe(xs, mixed, mods, mod_rows, seq_lens, gain, w_router, b_router, w_gate, w_up, w_down, layer,
         spare=None):
    d = xs[0].shape[1]
    w_router_t = w_router.T
    routed = [_router(x, a, mods, r0, sl, gain, w_router_t, b_router)
              for x, a, r0, sl in zip(xs, mixed, mod_rows, seq_lens)]
    flat_e = jnp.concatenate([eid.reshape(-1) for _, eid, _ in routed])
    n_assign = flat_e.shape[0]
    onehot = (flat_e[:, None] == jnp.arange(N_EXPERTS, dtype=jnp.int32)[None, :]).astype(jnp.int32)
    csum = jnp.cumsum(onehot, axis=0)
    counts = csum[-1]
    padded = (counts + MOE_ROWS - 1) // MOE_ROWS * MOE_ROWS
    pend = jnp.cumsum(padded)
    slot = jnp.sum(onehot * (csum - 1 + (pend - padded)[None, :]), axis=1)
    n_blocks = -(-n_assign // MOE_ROWS) + N_EXPERTS
    if spare is not None and spare.shape == (n_blocks * MOE_ROWS, d):
        buf = spare
    else:
        buf = jnp.zeros((n_blocks * MOE_ROWS, d), F32)
    block_start = jnp.arange(n_blocks, dtype=jnp.int32) * MOE_ROWS
    block_expert = jnp.minimum(jnp.sum((pend[None, :] <= block_start[:, None]).astype(jnp.int32), axis=1),
                               N_EXPERTS - 1)
    n_used = (pend[-1:] // MOE_ROWS).astype(jnp.int32)
    slot_blocks, off = [], 0
    for h2, _, _ in routed:
        n = h2.shape[0]
        sb = _slot_blocks(slot[off:off + TOP_K * n].reshape(TOP_K, n), min(MOE_TT, n))
        off += TOP_K * n
        slot_blocks.append(sb)
        buf = _dispatch(h2, sb, buf)
    yb = _expert_ffn(buf, block_expert, n_used, w_gate, w_up, w_down, layer)
    outs = [_combine(x, a, yb, sb, wts.T, mods, r0, sl)
            for x, a, (_, _, wts), sb, r0, sl in zip(xs, mixed, routed, slot_blocks, mod_rows, seq_lens)]
    return outs, buf


def _rope_tables(n_tok):
    t = jnp.arange(n_tok)
    row = (t // GRID_W).astype(F32)
    col = (t % GRID_W).astype(F32)
    quarter = HEAD_DIM // 4
    inv = ROPE_THETA ** (-jnp.arange(quarter, dtype=F32) / quarter)
    ar = row[:, None] * inv
    ac = col[:, None] * inv
    ang = jnp.concatenate([ar, ar, ac, ac], axis=-1)
    cos, sin = jnp.cos(ang), jnp.sin(ang)
    first = (jnp.arange(HEAD_DIM) % (2 * quarter)) < quarter
    return cos, jnp.where(first, -sin, 0.0), jnp.where(first, 0.0, sin)


def kernel(x_prompt, x_sample, cache_nat_k, cache_nat_v, cache_gqa_k, cache_gqa_v, state_ret_fwd,
           state_ret_bwd, c, c_ctx, w_mod, b_mod, norm1, norm2, w_in, conv_w, nat_qn, nat_kn, nat_rpb,
           gqa_qn, gqa_kn, ret_decay_fwd, ret_decay_bwd, ret_gn, w_branch, w_out, w_router, b_router,
           w_exp_gate, w_exp_up, w_exp_down):
    batch, seq, d = x_prompt.shape
    dec_batch, dec_seq, _ = x_sample.shape
    depth = w_mod.shape[0]
    n_heads = nat_rpb.shape[1]
    n_kv = cache_gqa_k.shape[2]
    hd = HEAD_DIM

    xp = x_prompt.reshape(batch * seq, d)
    xs = x_sample.reshape(dec_batch * dec_seq, d)
    cvecs = jnp.zeros((SUBLANES, d), F32).at[0].set(c_ctx).at[1:1 + dec_batch].set(c)
    mods_all = _modulation(cvecs, w_mod, b_mod).reshape(depth, SUBLANES, 6, d)
    rope_tabs = _rope_tables(dec_seq)
    zero_state = jnp.zeros((1, depth, 1, hd, hd), F32)

    w_in_b, wb, wo = w_in.astype(BF16), w_branch.astype(BF16), w_out.astype(BF16)
    wg, wu, wd = w_exp_gate.astype(BF16), w_exp_up.astype(BF16), w_exp_down.astype(BF16)
    gate_w = w_in.shape[2] - PROJ_W

    caches = [[] for _ in range(6)]
    moe_buf = None
    for l in range(depth):
        mods = mods_all[l]
        lg = jnp.stack([jax.nn.log_sigmoid(ret_decay_fwd[l].astype(F32)),
                        jax.nn.log_sigmoid(ret_decay_bwd[l].astype(F32))], axis=1)
        lg = jnp.broadcast_to(lg[:, :, None], (lg.shape[0], 2, hd))

        pc = _norm_project(xp, mods, 0, 0, norm1[l], w_in_b, l, 0, PROJ_W, gate=False, name="in_proj_ctx")
        gc = _norm_project(xp, mods, 0, 0, norm1[l], w_in_b, l, PROJ_W, gate_w, gate=True, name="gate_proj_ctx")
        conv_c = _short_conv(pc, conv_w[l], seq)
        nat_c, nk, nv = _attention(pc, seq, n_heads, 1, COL_NQ, COL_NK, COL_NV, nat_qn[l], nat_kn[l],
                                   emit_kv=True, name="nat_ctx")
        gqa_c, gk, gv = _attention(pc, seq, n_kv, GQA_GROUP, COL_GQ, COL_GK, COL_GV, gqa_qn[l], gqa_kn[l],
                                   emit_kv=True, name="gqa_ctx")
        ret_c, s_f, s_b = _retention(pc, seq, lg, ret_gn[l], zero_state, zero_state, l)
        for lst, val in zip(caches, (nk, nv, gk, gv, s_f, s_b)):
            lst.append(val)
        mix_c = _merge((conv_c, nat_c, gqa_c, ret_c), gc, wb, wo, l)

        pl_ = _norm_project(xs, mods, 1, dec_seq, norm1[l], w_in_b, l, 0, PROJ_W, gate=False, name="in_proj_lat")
        gl = _norm_project(xs, mods, 1, dec_seq, norm1[l], w_in_b, l, PROJ_W, gate_w, gate=True,
                           name="gate_proj_lat")
        conv_l = _short_conv(pl_, conv_w[l], dec_seq)
        nat_l = _natten(pl_, dec_seq, nat_qn[l], nat_kn[l], nat_rpb[l], cache_nat_k, cache_nat_v, l)
        gqa_l = _attention(pl_, dec_seq, n_kv, GQA_GROUP, COL_GQ, COL_GK, COL_GV, gqa_qn[l], gqa_kn[l],
                           rope_tabs=rope_tabs, cache=(cache_gqa_k, cache_gqa_v, l), name="gqa_lat")
        ret_l, _, _ = _retention(pl_, dec_seq, lg, ret_gn[l], state_ret_fwd, state_ret_bwd, l)
        mix_l = _merge((conv_l, nat_l, gqa_l, ret_l), gl, wb, wo, l)

        (xp, xs), moe_buf = _moe([xp, xs], [mix_c, mix_l], mods, [0, 1], [0, dec_seq], norm2[l], w_router,
                                 b_router, wg, wu, wd, l, spare=moe_buf)

    outs = [jnp.stack(v, axis=1) for v in caches]
    return (xp.reshape(batch, seq, d), xs.reshape(dec_batch, dec_seq, d), *outs)
```
